```python
import math
import jax, jax.numpy as jnp
from jax import lax
import numpy as np

D_MODEL = 1024
BATCH = 4
SEQ = 8192
DEPTH = 4

CTX_LEN = 256
GRID_W = 64
N_MIXERS = 2
N_LAYERS_A = (DEPTH + 1) // 2
N_LAYERS_B = DEPTH // 2
BRANCH_WIDTH = D_MODEL
DA_HEAD_DIM = 64
DA_HEADS = BRANCH_WIDTH // (2 * DA_HEAD_DIM)
DA_QK_WIDTH = DA_HEADS * 2 * DA_HEAD_DIM
DA_IN_WIDTH = 2 * DA_QK_WIDTH + 2 * BRANCH_WIDTH
GQ_HEAD_DIM = 128
GQ_HEADS = BRANCH_WIDTH // GQ_HEAD_DIM
GQ_KV_HEADS = 2
GQ_Q_WIDTH = GQ_HEADS * GQ_HEAD_DIM
GQ_KV_WIDTH = GQ_KV_HEADS * GQ_HEAD_DIM
GQ_IN_WIDTH = GQ_Q_WIDTH + 2 * GQ_KV_WIDTH + BRANCH_WIDTH
ROPE_THETA = 10000.0
Q_BLOCK = 128
NORM_EPS = 1e-6

kernel_name = "hybrid_diffattn_gqa_prefix_backbone"


def rms_norm(x, g):
    xf = x.astype(jnp.float32)
    y = xf * lax.rsqrt(jnp.mean(xf * xf, axis=-1, keepdims=True) + NORM_EPS)
    return (y * g.astype(jnp.float32)).astype(x.dtype)


def adaln(cvec, w, b):
    m = jax.nn.silu(cvec) @ w + b
    return jnp.split(m, 3, axis=-1)


def axial_rope_tables(rows, cols, head_dim):
    axis_dim = head_dim // 2
    freqs = ROPE_THETA ** (-jnp.arange(0, axis_dim, 2, dtype=jnp.float32) / axis_dim)
    ang = jnp.concatenate([rows.astype(jnp.float32)[:, None] * freqs,
                           cols.astype(jnp.float32)[:, None] * freqs], axis=-1)
    return jnp.cos(ang), jnp.sin(ang)


def apply_rope(x, cos, sin):
    S, half = cos.shape
    shp = (1, S) + (1,) * (x.ndim - 3) + (half,)
    cs = cos.reshape(shp).astype(x.dtype)
    sn = sin.reshape(shp).astype(x.dtype)
    x1, x2 = jnp.split(x, 2, axis=-1)
    return jnp.concatenate([x1 * cs - x2 * sn, x2 * cs + x1 * sn], axis=-1)


def sweep_attention(q, k, v):
    B, Sq, Hq, Dh = q.shape
    Hkv = k.shape[2]
    G = Hq // Hkv
    Dv = v.shape[-1]
    scale = Dh ** -0.5
    k32 = k.astype(jnp.float32)
    qb = q.reshape(B, Sq // Q_BLOCK, Q_BLOCK, Hkv, G, Dh).transpose(1, 0, 2, 3, 4, 5)

    def one_block(qblk):
        s = jnp.einsum('bqhgd,bkhd->bhgqk', qblk.astype(jnp.float32), k32) * scale
        p = jax.nn.softmax(s, axis=-1)
        return jnp.einsum('bhgqk,bkhd->bqhgd', p.astype(v.dtype), v)

    out = lax.map(one_block, qb)
    return out.transpose(1, 0, 2, 3, 4, 5).reshape(B, Sq, Hq, Dv)


def lambda_init_fn(layer_idx):
    return 0.8 - 0.6 * math.exp(-0.3 * layer_idx)


def diff_attention_mixer(h, hc, w_in, w_out, lam_p, subln_g, lam_init, cos, sin, with_ctx_out):
    B, S, _ = h.shape

    def project(t):
        L = t.shape[1]
        p = t @ w_in
        q, k, v, z = jnp.split(p, [DA_QK_WIDTH, 2 * DA_QK_WIDTH, 2 * DA_QK_WIDTH + BRANCH_WIDTH], axis=-1)
        q = q.reshape(B, L, DA_HEADS, 2, DA_HEAD_DIM)
        k = k.reshape(B, L, DA_HEADS, 2, DA_HEAD_DIM)
        v = v.reshape(B, L, DA_HEADS, 2 * DA_HEAD_DIM)
        return q, k, v, z

    q, k, v, z = project(h)
    qc, kc, vc, zc = project(hc)
    q = apply_rope(q, cos, sin)
    k = apply_rope(k, cos, sin)
    k_all = jnp.concatenate([k, kc], axis=1)
    v_all = jnp.concatenate([v, vc], axis=1)

    lp = lam_p.astype(jnp.float32)
    lam = jnp.exp(jnp.sum(lp[0] * lp[1])) - jnp.exp(jnp.sum(lp[2] * lp[3])) + lam_init

    def diff(qq, kk, vv, gate):
        o1 = sweep_attention(qq[..., 0, :], kk[..., 0, :], vv)
        o2 = sweep_attention(qq[..., 1, :], kk[..., 1, :], vv)
        o = o1 - lam.astype(o1.dtype) * o2
        o = rms_norm(o, subln_g) * (1.0 - lam_init)
        o = o.reshape(o.shape[0], o.shape[1], BRANCH_WIDTH)
        return (o * jax.nn.silu(gate)) @ w_out

    y = diff(q, k_all, v_all, z)
    yc = diff(qc, kc, vc, zc) if with_ctx_out else None
    return y, yc


def gqa_mixer(h, hc, w_in, w_out, qk_g, cos, sin, with_ctx_out):
    B, S, _ = h.shape

    def project(t):
        L = t.shape[1]
        p = t @ w_in
        q, k, v, z = jnp.split(p, [GQ_Q_WIDTH, GQ_Q_WIDTH + GQ_KV_WIDTH, GQ_Q_WIDTH + 2 * GQ_KV_WIDTH], axis=-1)
        q = rms_norm(q.reshape(B, L, GQ_HEADS, GQ_HEAD_DIM), qk_g[0])
        k = rms_norm(k.reshape(B, L, GQ_KV_HEADS, GQ_HEAD_DIM), qk_g[1])
        v = v.reshape(B, L, GQ_KV_HEADS, GQ_HEAD_DIM)
        return q, k, v, z

    q, k, v, z = project(h)
    qc, kc, vc, zc = project(hc)
    q = apply_rope(q, cos, sin)
    k = apply_rope(k, cos, sin)
    k_all = jnp.concatenate([k, kc], axis=1)
    v_all = jnp.concatenate([v, vc], axis=1)

    def attend(qq, kk, vv, gate):
        o = sweep_attention(qq, kk, vv)
        o = o.reshape(o.shape[0], o.shape[1], BRANCH_WIDTH)
        return (o * jax.nn.silu(gate)) @ w_out

    y = attend(q, k_all, v_all, z)
    yc = attend(qc, kc, vc, zc) if with_ctx_out else None
    return y, yc


def setup_inputs(seed: int = 0) -> dict:
    key = jax.random.key(seed)
    ks = jax.random.split(key, 16)
    f32 = jnp.float32
    D = D_MODEL
    s_in = D ** -0.5
    return {
        "x": jax.random.normal(ks[0], (BATCH, SEQ, D), f32),
        "c": jax.random.normal(ks[1], (BATCH, D), f32),
        "ctx": jax.random.normal(ks[2], (BATCH, CTX_LEN, D), f32),
        "c_ctx": jax.random.normal(ks[3], (D,), f32),
        "ada_w": jax.random.normal(ks[4], (DEPTH, D, 3 * D), f32) * (0.5 * s_in),
        "ada_b": jax.random.normal(ks[5], (DEPTH, 3 * D), f32) * 0.01,
        "pre_g": 1.0 + 0.05 * jax.random.normal(ks[6], (DEPTH, D), f32),
        "post_g": 1.0 + 0.05 * jax.random.normal(ks[7], (DEPTH, D), f32),
        "w_out": jax.random.normal(ks[8], (DEPTH, BRANCH_WIDTH, D), f32) * (BRANCH_WIDTH ** -0.5),
        "a_w_in": jax.random.normal(ks[9], (N_LAYERS_A, D, DA_IN_WIDTH), f32) * s_in,
        "a_lambda": jax.random.normal(ks[10], (N_LAYERS_A, 4, DA_HEAD_DIM), f32) * 0.1,
        "a_subln_g": 1.0 + 0.05 * jax.random.normal(ks[11], (N_LAYERS_A, 2 * DA_HEAD_DIM), f32),
        "b_w_in": jax.random.normal(ks[12], (N_LAYERS_B, D, GQ_IN_WIDTH), f32) * s_in,
        "b_qk_g": 1.0 + 0.05 * jax.random.normal(ks[13], (N_LAYERS_B, 2, GQ_HEAD_DIM), f32),
    }


def reference(x, c, ctx, c_ctx, ada_w, ada_b, pre_g, post_g, w_out, a_w_in, a_lambda, a_subln_g, b_w_in, b_qk_g):
    n_tokens = x.shape[1]
    n_rows = n_tokens // GRID_W
    rows = jnp.repeat(jnp.arange(n_rows, dtype=jnp.int32), GRID_W)
    cols = jnp.tile(jnp.arange(GRID_W, dtype=jnp.int32), n_rows)
    cos_a, sin_a = axial_rope_tables(rows, cols, DA_HEAD_DIM)
    cos_b, sin_b = axial_rope_tables(rows, cols, GQ_HEAD_DIM)

    xc = ctx
    for i in range(DEPTH):
        last = i == DEPTH - 1
        shift, scale, gate = adaln(c, ada_w[i], ada_b[i])
        shift, scale, gate = shift[:, None, :], scale[:, None, :], gate[:, None, :]
        cshift, cscale, cgate = adaln(c_ctx, ada_w[i], ada_b[i])
        h = rms_norm(x, pre_g[i]) * (1.0 + scale) + shift
        hc = rms_norm(xc, pre_g[i]) * (1.0 + cscale) + cshift
        j = i // N_MIXERS
        if i % N_MIXERS == 0:
            y, yc = diff_attention_mixer(h, hc, a_w_in[j], w_out[i], a_lambda[j], a_subln_g[j],
                                         lambda_init_fn(i), cos_a, sin_a, not last)
        else:
            y, yc = gqa_mixer(h, hc, b_w_in[j], w_out[i], b_qk_g[j], cos_b, sin_b, not last)
        x = x + gate * rms_norm(y, post_g[i])
        if not last:
            xc = xc + cgate * rms_norm(yc, post_g[i])
    return x
```

```python
import functools
import math

import jax
import jax.numpy as jnp
from jax import lax
from jax.experimental import pallas as pl
from jax.experimental.pallas import tpu as pltpu

D_MODEL = 1024
CTX_LEN = 256
GRID_W = 64
DA_HEAD_DIM = 64
DA_HEADS = 8
GQ_HEAD_DIM = 128
GQ_HEADS = 8
GQ_KV_HEADS = 2
ROPE_THETA = 10000.0
NORM_EPS = 1e-6

LANES = 128
TOKEN_TILE = 256
Q_TILE = 512
KV_CHUNK = 768
VMEM_LIMIT = 48 * 1024 * 1024
LOG2E = math.log2(math.e)

_NT = (((1,), (1,)), ((), ()))


def _lambda_init(layer_idx):
    return 0.8 - 0.6 * math.exp(-0.3 * layer_idx)


def _silu(x):
    return x / (1.0 + jnp.exp(-x))


def _rms(x):
    return x * lax.rsqrt(jnp.mean(x * x, axis=-1, keepdims=True) + NORM_EPS)


def _mods_kernel(cv_ref, w_ref, b_ref, o_ref):
    sc = _silu(cv_ref[...])
    o_ref[0] = jnp.dot(sc, w_ref[0], precision=lax.Precision.HIGHEST,
                       preferred_element_type=jnp.float32) + b_ref[0]


def _mods(cvecs, ada_w, ada_b):
    depth, d, d3 = ada_w.shape
    rows = cvecs.shape[0]
    tn = 1024
    return pl.pallas_call(
        _mods_kernel,
        grid=(depth, d3 // tn),
        in_specs=[
            pl.BlockSpec((rows, d), lambda i, j: (0, 0)),
            pl.BlockSpec((1, d, tn), lambda i, j: (i, 0, j)),
            pl.BlockSpec((1, 1, tn), lambda i, j: (i, 0, j)),
        ],
        out_specs=pl.BlockSpec((1, rows, tn), lambda i, j: (i, 0, j)),
        out_shape=jax.ShapeDtypeStruct((depth, rows, d3), jnp.float32),
        compiler_params=pltpu.CompilerParams(
            dimension_semantics=("arbitrary", "arbitrary"),
            vmem_limit_bytes=VMEM_LIMIT),
        name="adaln_mods",
    )(cvecs, ada_w, ada_b.reshape(depth, 1, d3))


def _rope_chunk(xc, cos, sin_signed, first_half, half):
    if 2 * half == LANES:
        partner = pltpu.roll(xc, half, 1)
    else:
        partner = jnp.where(first_half, pltpu.roll(xc, LANES - half, 1), pltpu.roll(xc, half, 1))
    return xc * cos + partner * sin_signed


def _proj_kernel(*refs, diff, q_scale):
    if diff:
        (x_ref, mod_ref, pg_ref, w_ref, cos_ref, sin_ref,
         q1_ref, q2_ref, k_ref, v_ref, gz_ref) = refs
    else:
        (x_ref, mod_ref, pg_ref, w_ref, cos_ref, sin_ref, qkg_ref,
         q_ref, k_ref, v_ref, gz_ref) = refs
    d = D_MODEL
    x = x_ref[0]
    mod = mod_ref[0]
    shift, scale = mod[:, 0:d], mod[:, d:2 * d]
    h = _rms(x) * pg_ref[...] * (1.0 + scale) + shift
    hb = h.astype(jnp.bfloat16)

    cos = cos_ref[...]
    sin = sin_ref[...]
    lane = lax.broadcasted_iota(jnp.int32, (1, LANES), 1)
    if diff:
        half = DA_HEAD_DIM // 2
        first_half = (lane % DA_HEAD_DIM) < half
        lo = lane < DA_HEAD_DIM
        nq, nk, nv = DA_HEADS, DA_HEADS, DA_HEADS
    else:
        half = GQ_HEAD_DIM // 2
        first_half = None
        nq, nk, nv = GQ_HEADS, GQ_KV_HEADS, GQ_KV_HEADS
    col = 0

    pq = jnp.dot(hb, w_ref[:, col:col + nq * LANES], preferred_element_type=jnp.float32)
    col += nq * LANES
    for c in range(nq):
        xc = pq[:, c * LANES:(c + 1) * LANES]
        if not diff:
            xc = _rms(xc) * qkg_ref[0:1, :]
        r = _rope_chunk(xc, cos, sin, first_half, half) * q_scale
        sl = slice(c * LANES, (c + 1) * LANES)
        if diff:
            q1_ref[0, :, sl] = jnp.where(lo, r, 0.0).astype(jnp.bfloat16)
            q2_ref[0, :, sl] = jnp.where(lo, 0.0, r).astype(jnp.bfloat16)
        else:
            q_ref[0, :, sl] = r.astype(jnp.bfloat16)

    pk = jnp.dot(hb, w_ref[:, col:col + nk * LANES], preferred_element_type=jnp.float32)
    col += nk * LANES
    for c in range(nk):
        xc = pk[:, c * LANES:(c + 1) * LANES]
        if not diff:
            xc = _rms(xc) * qkg_ref[1:2, :]
        r = _rope_chunk(xc, cos, sin, first_half, half)
        k_ref[0, :, c * LANES:(c + 1) * LANES] = r.astype(jnp.bfloat16)

    pv = jnp.dot(hb, w_ref[:, col:col + nv * LANES], preferred_element_type=jnp.float32)
    col += nv * LANES
    v_ref[0] = pv.astype(jnp.bfloat16)

    pz = jnp.dot(hb, w_ref[:, col:col + d], preferred_element_type=jnp.float32)
    gz_ref[0] = _silu(pz).astype(jnp.bfloat16)


def _mod_index(n_lat_tiles, n_batch):
    def index(b, t):
        return (jnp.where(t < n_lat_tiles, b, n_batch), 0, 0)
    return index


def _project(xs, mods_i, pre_g_i, w_in_bf16, cos, sin, qk_g, *, diff, n_lat_tiles):
    bsz, n_tok, d = xs.shape
    tm = TOKEN_TILE
    width = w_in_bf16.shape[1]
    q_scale = (DA_HEAD_DIM if diff else GQ_HEAD_DIM) ** -0.5 * LOG2E
    tok = lambda w: pl.BlockSpec((1, tm, w), lambda b, t: (b, t, 0))
    in_specs = [
        tok(d),
        pl.BlockSpec((1, 1, 3 * d), _mod_index(n_lat_tiles, bsz)),
        pl.BlockSpec((1, d), lambda b, t: (0, 0)),
        pl.BlockSpec((d, width), lambda b, t: (0, 0)),
        pl.BlockSpec((tm, LANES), lambda b, t: (t, 0)),
        pl.BlockSpec((tm, LANES), lambda b, t: (t, 0)),
    ]
    args = [xs, mods_i, pre_g_i.reshape(1, d), w_in_bf16, cos, sin]
    bf = lambda w: jax.ShapeDtypeStruct((bsz, n_tok, w), jnp.bfloat16)
    if diff:
        kvw = DA_HEADS * LANES
        out_shape = [bf(d), bf(d), bf(kvw), bf(kvw), bf(d)]
        out_specs = [tok(d), tok(d), tok(kvw), tok(kvw), tok(d)]
    else:
        kvw = GQ_KV_HEADS * LANES
        in_specs.append(pl.BlockSpec((2, LANES), lambda b, t: (0, 0)))
        args.append(qk_g)
        out_shape = [bf(d), bf(kvw), bf(kvw), bf(d)]
        out_specs = [tok(d), tok(kvw), tok(kvw), tok(d)]
    return pl.pallas_call(
        functools.partial(_proj_kernel, diff=diff, q_scale=q_scale),
        grid=(bsz, n_tok // tm),
        in_specs=in_specs,
        out_specs=out_specs,
        out_shape=out_shape,
        compiler_params=pltpu.CompilerParams(
            dimension_semantics=("arbitrary", "arbitrary"),
            vmem_limit_bytes=VMEM_LIMIT),
        name="proj_diff" if diff else "proj_gqa",
    )(*args)


def _attn_kernel(*refs, diff, kv_chunk, n_chunks, lam_init):
    if diff:
        (q1_ref, q2_ref, k_ref, v_ref, gz_ref, lam_ref, g_ref,
         o_ref, m_ref, l_ref, acc_ref) = refs
        q_refs = (q1_ref, q2_ref)
    else:
        q_ref, k_ref, v_ref, gz_ref, o_ref, m_ref, l_ref, acc_ref = refs
        q_refs = (q_ref,)

    m_ref[...] = jnp.full(m_ref.shape, -jnp.inf, jnp.float32)
    l_ref[...] = jnp.zeros(l_ref.shape, jnp.float32)
    acc_ref[...] = jnp.zeros(acc_ref.shape, jnp.float32)

    def chunk(j, carry):
        off = pl.multiple_of(j * kv_chunk, kv_chunk)
        k = k_ref[0, pl.ds(off, kv_chunk), :]
        v = v_ref[0, pl.ds(off, kv_chunk), :]
        for t, q_ref_t in enumerate(q_refs):
            s = lax.dot_general(q_ref_t[0], k, _NT, preferred_element_type=jnp.float32)
            m_old = m_ref[t]
            m_new = jnp.maximum(m_old, jnp.max(s, axis=-1, keepdims=True))
            alpha = jnp.exp2(m_old - m_new)
            p = jnp.exp2(s - m_new)
            l_ref[t] = alpha * l_ref[t] + jnp.sum(p, axis=-1, keepdims=True)
            acc_ref[t] = alpha * acc_ref[t] + jnp.dot(
                p.astype(jnp.bfloat16), v, preferred_element_type=jnp.float32)
            m_ref[t] = m_new
        return carry

    lax.fori_loop(0, n_chunks, chunk, 0)

    if diff:
        lp = lam_ref[...]
        lam = (jnp.exp(jnp.sum(lp[0:1] * lp[1:2], axis=-1, keepdims=True))
               - jnp.exp(jnp.sum(lp[2:3] * lp[3:4], axis=-1, keepdims=True)) + lam_init)
        o = acc_ref[0] / l_ref[0] - lam * (acc_ref[1] / l_ref[1])
        o = _rms(o) * g_ref[...] * (1.0 - lam_init)
    else:
        o = acc_ref[0] / l_ref[0]
    o_ref[0] = (o * gz_ref[0].astype(jnp.float32)).astype(jnp.bfloat16)


def _attention(qs, k, v, gz, lam_p, subln_g, *, diff, lam_init, q_tile, q_off_tiles,
               n_q_tiles, kv_len, kv_off_blocks, kv_chunk, name):
    bsz, n_tok, d = gz.shape
    n_heads = DA_HEADS if diff else GQ_HEADS
    group = 1 if diff else GQ_HEADS // GQ_KV_HEADS
    n_soft = 2 if diff else 1
    qspec = pl.BlockSpec((1, q_tile, LANES), lambda b, h, i: (b, q_off_tiles + i, h))
    kvspec = pl.BlockSpec((1, kv_len, LANES), lambda b, h, i: (b, kv_off_blocks, h // group))
    in_specs = [qspec] * n_soft + [kvspec, kvspec, qspec]
    args = list(qs) + [k, v, gz]
    if diff:
        in_specs += [pl.BlockSpec((4, DA_HEAD_DIM), lambda b, h, i: (0, 0)),
                     pl.BlockSpec((1, LANES), lambda b, h, i: (0, 0))]
        args += [lam_p, subln_g.reshape(1, LANES)]
    return pl.pallas_call(
        functools.partial(_attn_kernel, diff=diff, kv_chunk=kv_chunk,
                          n_chunks=kv_len // kv_chunk, lam_init=lam_init),
        grid=(bsz, n_heads, n_q_tiles),
        in_specs=in_specs,
        out_specs=pl.BlockSpec((1, q_tile, LANES), lambda b, h, i: (b, i, h)),
        out_shape=jax.ShapeDtypeStruct((bsz, n_q_tiles * q_tile, d), jnp.bfloat16),
        scratch_shapes=[
            pltpu.VMEM((n_soft, q_tile, 1), jnp.float32),
            pltpu.VMEM((n_soft, q_tile, 1), jnp.float32),
            pltpu.VMEM((n_soft, q_tile, LANES), jnp.float32),
        ],
        compiler_params=pltpu.CompilerParams(
            dimension_semantics=("arbitrary", "arbitrary", "arbitrary"),
            vmem_limit_bytes=VMEM_LIMIT),
        name=name,
    )(*args)


def _out_kernel(og_ref, w_ref, mod_ref, pg_ref, x_ref, o_ref):
    d = D_MODEL
    y = jnp.dot(og_ref[0], w_ref[...], preferred_element_type=jnp.float32)
    gate = mod_ref[0][:, 2 * d:3 * d]
    o_ref[0] = x_ref[0] + gate * (_rms(y) * pg_ref[...])


def _out_project(og, w_out_bf16, mods_i, post_g_i, xs, *, n_tok_out, n_lat_tiles):
    bsz, _, d = xs.shape
    tm = TOKEN_TILE
    tok = pl.BlockSpec((1, tm, d), lambda b, t: (b, t, 0))
    return pl.pallas_call(
        _out_kernel,
        grid=(bsz, n_tok_out // tm),
        in_specs=[
            tok,
            pl.BlockSpec((d, d), lambda b, t: (0, 0)),
            pl.BlockSpec((1, 1, 3 * d), _mod_index(n_lat_tiles, bsz)),
            pl.BlockSpec((1, d), lambda b, t: (0, 0)),
            tok,
        ],
        out_specs=tok,
        out_shape=jax.ShapeDtypeStruct((bsz, n_tok_out, d), jnp.float32),
        compiler_params=pltpu.CompilerParams(
            dimension_semantics=("arbitrary", "arbitrary"),
            vmem_limit_bytes=VMEM_LIMIT),
        name="out_proj",
    )(og, w_out_bf16, mods_i, post_g_i.reshape(1, d), xs)


def _rope_tables(n_lat, n_ctx, head_dim):
    rows = jnp.repeat(jnp.arange(n_lat // GRID_W, dtype=jnp.int32), GRID_W)
    cols = jnp.tile(jnp.arange(GRID_W, dtype=jnp.int32), n_lat // GRID_W)
    axis_dim = head_dim // 2
    freqs = ROPE_THETA ** (-jnp.arange(0, axis_dim, 2, dtype=jnp.float32) / axis_dim)
    ang = jnp.concatenate([rows.astype(jnp.float32)[:, None] * freqs,
                           cols.astype(jnp.float32)[:, None] * freqs], axis=-1)
    cos, sin = jnp.cos(ang), jnp.sin(ang)
    reps = LANES // head_dim
    cos_w = jnp.tile(jnp.concatenate([cos, cos], axis=-1), (1, reps))
    sin_w = jnp.tile(jnp.concatenate([-sin, sin], axis=-1), (1, reps))
    cos_w = jnp.concatenate([cos_w, jnp.ones((n_ctx, LANES), jnp.float32)], axis=0)
    sin_w = jnp.concatenate([sin_w, jnp.zeros((n_ctx, LANES), jnp.float32)], axis=0)
    return cos_w, sin_w


def kernel(x, c, ctx, c_ctx, ada_w, ada_b, pre_g, post_g, w_out, a_w_in, a_lambda, a_subln_g, b_w_in, b_qk_g):
    bsz, n_lat, d = x.shape
    n_ctx = ctx.shape[1]
    depth = ada_w.shape[0]
    n_tok = n_lat + n_ctx
    assert d == D_MODEL and n_ctx == CTX_LEN
    assert n_lat % Q_TILE == 0 and n_tok % KV_CHUNK == 0 and n_lat % n_ctx == 0
    n_lat_tiles = n_lat // TOKEN_TILE

    cvecs = jnp.concatenate([c, c_ctx[None, :], jnp.zeros((7 - bsz, d), jnp.float32)], axis=0)
    mods = _mods(cvecs, ada_w, ada_b).reshape(depth, 8, 1, 3 * d)

    cos_a, sin_a = _rope_tables(n_lat, n_ctx, DA_HEAD_DIM)
    cos_b, sin_b = _rope_tables(n_lat, n_ctx, GQ_HEAD_DIM)
    w_out_bf = w_out.astype(jnp.bfloat16)
    a_w_bf = a_w_in.astype(jnp.bfloat16)
    b_w_bf = b_w_in.astype(jnp.bfloat16)

    xs = jnp.concatenate([x, ctx], axis=1)
    for i in range(depth):
        last = i == depth - 1
        diff = i % 2 == 0
        j = i // 2
        lam_init = _lambda_init(i)
        if diff:
            q1, q2, k, v, gz = _project(xs, mods[i], pre_g[i], a_w_bf[j], cos_a, sin_a, None,
                                        diff=True, n_lat_tiles=n_lat_tiles)
            qs, lam_p, sub_g = (q1, q2), a_lambda[j], a_subln_g[j]
        else:
            q, k, v, gz = _project(xs, mods[i], pre_g[i], b_w_bf[j], cos_b, sin_b, b_qk_g[j],
                                   diff=False, n_lat_tiles=n_lat_tiles)
            qs, lam_p, sub_g = (q,), None, None
        og = _attention(qs, k, v, gz, lam_p, sub_g, diff=diff, lam_init=lam_init,
                        q_tile=Q_TILE, q_off_tiles=0, n_q_tiles=n_lat // Q_TILE,
                        kv_len=n_tok, kv_off_blocks=0, kv_chunk=KV_CHUNK,
                        name="attn_diff" if diff else "attn_gqa")
        if last:
            return _out_project(og, w_out_bf[i], mods[i], post_g[i], xs,
                                n_tok_out=n_lat, n_lat_tiles=n_lat_tiles)
        ogc = _attention(qs, k, v, gz, lam_p, sub_g, diff=diff, lam_init=lam_init,
                         q_tile=n_ctx, q_off_tiles=n_lat // n_ctx, n_q_tiles=1,
                         kv_len=n_ctx, kv_off_blocks=n_lat // n_ctx, kv_chunk=n_ctx,
                         name="attn_diff_ctx" if diff else "attn_gqa_ctx")
        og_all = jnp.concatenate([og, ogc], axis=1)
        xs = _out_project(og_all, w_out_bf[i], mods[i], post_g[i], xs,
                          n_tok_out=n_tok, n_lat_tiles=n_lat_tiles)
    return xs[:, :n_lat]
```

```python
import functools
import math

import jax
import jax.numpy as jnp
from jax import lax
from jax.experimental import pallas as pl
from jax.experimental.pallas import tpu as pltpu

D_MODEL = 1024
CTX_LEN = 256
GRID_W = 64
DA_HEAD_DIM = 64
DA_HEADS = 8
GQ_HEAD_DIM = 128
GQ_HEADS = 8
GQ_KV_HEADS = 2
ROPE_THETA = 10000.0
NORM_EPS = 1e-6

LANES = 128
TOKEN_TILE = 256
SOFTMAX_COLS = 512
Q_ROWS_DIFF = SOFTMAX_COLS // 2
Q_ROWS_GQA = SOFTMAX_COLS // (GQ_HEADS // GQ_KV_HEADS)
KV_CHUNK = 768
VMEM_LIMIT = 48 * 1024 * 1024
LOG2E = math.log2(math.e)

_NT = (((1,), (1,)), ((), ()))


def _lambda_init(layer_idx):
    return 0.8 - 0.6 * math.exp(-0.3 * layer_idx)


def _silu(x):
    return x / (1.0 + jnp.exp(-x))


def _rms(x):
    return x * lax.rsqrt(jnp.mean(x * x, axis=-1, keepdims=True) + NORM_EPS)


def _mods_kernel(cv_ref, w_ref, b_ref, o_ref):
    sc = _silu(cv_ref[...])
    o_ref[0] = jnp.dot(sc, w_ref[0], precision=lax.Precision.HIGHEST,
                       preferred_element_type=jnp.float32) + b_ref[0]


def _mods(cvecs, ada_w, ada_b):
    depth, d, d3 = ada_w.shape
    rows = cvecs.shape[0]
    tn = 1024
    return pl.pallas_call(
        _mods_kernel,
        grid=(depth, d3 // tn),
        in_specs=[
            pl.BlockSpec((rows, d), lambda i, j: (0, 0)),
            pl.BlockSpec((1, d, tn), lambda i, j: (i, 0, j)),
            pl.BlockSpec((1, 1, tn), lambda i, j: (i, 0, j)),
        ],
        out_specs=pl.BlockSpec((1, rows, tn), lambda i, j: (i, 0, j)),
        out_shape=jax.ShapeDtypeStruct((depth, rows, d3), jnp.float32),
        compiler_params=pltpu.CompilerParams(
            dimension_semantics=("arbitrary", "arbitrary"),
            vmem_limit_bytes=VMEM_LIMIT),
        name="adaln_mods",
    )(cvecs, ada_w, ada_b.reshape(depth, 1, d3))


def _rope_chunk(xc, cos, sin_signed, first_half, half):
    if 2 * half == LANES:
        partner = pltpu.roll(xc, half, 1)
    else:
        partner = jnp.where(first_half, pltpu.roll(xc, LANES - half, 1), pltpu.roll(xc, half, 1))
    return xc * cos + partner * sin_signed


def _proj_kernel(*refs, diff, q_scale):
    if diff:
        (x_ref, mod_ref, pg_ref, w_ref, cos_ref, sin_ref,
         q1_ref, q2_ref, k_ref, v_ref, gz_ref) = refs
    else:
        (x_ref, mod_ref, pg_ref, w_ref, cos_ref, sin_ref, qkg_ref,
         q_ref, k_ref, v_ref, gz_ref) = refs
    d = D_MODEL
    x = x_ref[0]
    mod = mod_ref[0]
    shift, scale = mod[:, 0:d], mod[:, d:2 * d]
    h = _rms(x) * pg_ref[...] * (1.0 + scale) + shift
    hb = h.astype(jnp.bfloat16)

    cos = cos_ref[...]
    sin = sin_ref[...]
    lane = lax.broadcasted_iota(jnp.int32, (1, LANES), 1)
    if diff:
        half = DA_HEAD_DIM // 2
        first_half = (lane % DA_HEAD_DIM) < half
        lo = lane < DA_HEAD_DIM
        nq, nk, nv = DA_HEADS, DA_HEADS, DA_HEADS
    else:
        half = GQ_HEAD_DIM // 2
        first_half = None
        nq, nk, nv = GQ_HEADS, GQ_KV_HEADS, GQ_KV_HEADS
    col = 0

    pq = jnp.dot(hb, w_ref[:, col:col + nq * LANES], preferred_element_type=jnp.float32)
    col += nq * LANES
    for c in range(nq):
        xc = pq[:, c * LANES:(c + 1) * LANES]
        if not diff:
            xc = _rms(xc) * qkg_ref[0:1, :]
        r = _rope_chunk(xc, cos, sin, first_half, half) * q_scale
        sl = slice(c * LANES, (c + 1) * LANES)
        if diff:
            q1_ref[0, :, sl] = jnp.where(lo, r, 0.0).astype(jnp.bfloat16)
            q2_ref[0, :, sl] = jnp.where(lo, 0.0, r).astype(jnp.bfloat16)
        else:
            q_ref[0, :, sl] = r.astype(jnp.bfloat16)

    pk = jnp.dot(hb, w_ref[:, col:col + nk * LANES], preferred_element_type=jnp.float32)
    col += nk * LANES
    for c in range(nk):
        xc = pk[:, c * LANES:(c + 1) * LANES]
        if not diff:
            xc = _rms(xc) * qkg_ref[1:2, :]
        r = _rope_chunk(xc, cos, sin, first_half, half)
        k_ref[0, :, c * LANES:(c + 1) * LANES] = r.astype(jnp.bfloat16)

    pv = jnp.dot(hb, w_ref[:, col:col + nv * LANES], preferred_element_type=jnp.float32)
    col += nv * LANES
    v_ref[0, 0] = pv.T.astype(jnp.bfloat16)

    pz = jnp.dot(hb, w_ref[:, col:col + d], preferred_element_type=jnp.float32)
    gz_ref[0] = _silu(pz).astype(jnp.bfloat16)


def _mod_index(n_lat_tiles, n_batch):
    def index(b, t):
        return (jnp.where(t < n_lat_tiles, b, n_batch), 0, 0)
    return index


def _project(xs, mods_i, pre_g_i, w_in_bf16, cos, sin, qk_g, *, diff, n_lat_tiles):
    bsz, n_tok, d = xs.shape
    tm = TOKEN_TILE
    width = w_in_bf16.shape[1]
    q_scale = (DA_HEAD_DIM if diff else GQ_HEAD_DIM) ** -0.5 * LOG2E
    tok = lambda w: pl.BlockSpec((1, tm, w), lambda b, t: (b, t, 0))
    in_specs = [
        tok(d),
        pl.BlockSpec((1, 1, 3 * d), _mod_index(n_lat_tiles, bsz)),
        pl.BlockSpec((1, d), lambda b, t: (0, 0)),
        pl.BlockSpec((d, width), lambda b, t: (0, 0)),
        pl.BlockSpec((tm, LANES), lambda b, t: (t, 0)),
        pl.BlockSpec((tm, LANES), lambda b, t: (t, 0)),
    ]
    args = [xs, mods_i, pre_g_i.reshape(1, d), w_in_bf16, cos, sin]
    bf = lambda w: jax.ShapeDtypeStruct((bsz, n_tok, w), jnp.bfloat16)
    kvw = (DA_HEADS if diff else GQ_KV_HEADS) * LANES
    tiles_per_chunk = KV_CHUNK // tm
    vt_shape = jax.ShapeDtypeStruct((bsz, n_tok // KV_CHUNK, kvw, KV_CHUNK), jnp.bfloat16)
    vt_spec = pl.BlockSpec((1, 1, kvw, tm),
                           lambda b, t: (b, t // tiles_per_chunk, 0, t % tiles_per_chunk))
    if diff:
        out_shape = [bf(d), bf(d), bf(kvw), vt_shape, bf(d)]
        out_specs = [tok(d), tok(d), tok(kvw), vt_spec, tok(d)]
    else:
        in_specs.append(pl.BlockSpec((2, LANES), lambda b, t: (0, 0)))
        args.append(qk_g)
        out_shape = [bf(d), bf(kvw), vt_shape, bf(d)]
        out_specs = [tok(d), tok(kvw), vt_spec, tok(d)]
    return pl.pallas_call(
        functools.partial(_proj_kernel, diff=diff, q_scale=q_scale),
        grid=(bsz, n_tok // tm),
        in_specs=in_specs,
        out_specs=out_specs,
        out_shape=out_shape,
        compiler_params=pltpu.CompilerParams(
            dimension_semantics=("arbitrary", "arbitrary"),
            vmem_limit_bytes=VMEM_LIMIT),
        name="proj_diff" if diff else "proj_gqa",
    )(*args)


def _attn_kernel(*refs, diff, n_chunks, lam_init):
    if diff:
        (q1_ref, q2_ref, k_ref, vt_ref, gz_ref, lam_ref, g_ref,
         o_ref, qs_ref, s_ref, m_ref, l_ref, acc_ref) = refs
        rows = q1_ref.shape[1]
        qs_ref[0:rows] = q1_ref[0]
        qs_ref[rows:2 * rows] = q2_ref[0]
    else:
        (q_ref, k_ref, vt_ref, gz_ref, o_ref, qs_ref, s_ref, m_ref, l_ref, acc_ref) = refs
        rows = q_ref.shape[1]
        for h in range(GQ_HEADS // GQ_KV_HEADS):
            qs_ref[h * rows:(h + 1) * rows] = q_ref[0, :, h * LANES:(h + 1) * LANES]
    kv_chunk = vt_ref.shape[3]

    m_ref[...] = jnp.full(m_ref.shape, -jnp.inf, jnp.float32)
    l_ref[...] = jnp.zeros(l_ref.shape, jnp.float32)
    acc_ref[...] = jnp.zeros(acc_ref.shape, jnp.float32)

    def scores(j, slot):
        off = j * kv_chunk
        if not isinstance(off, int):
            off = pl.multiple_of(off, kv_chunk)
        s_ref[slot] = lax.dot_general(k_ref[0, pl.ds(off, kv_chunk), :], qs_ref[...], _NT,
                                      preferred_element_type=jnp.float32)

    def softmax_pv(j, slot):
        s = s_ref[slot]
        m_old = m_ref[...]
        m_new = jnp.maximum(m_old, jnp.max(s, axis=0, keepdims=True))
        alpha = jnp.exp2(m_old - m_new)
        p = jnp.exp2(s - m_new)
        l_ref[...] = alpha * l_ref[...] + jnp.sum(p, axis=0, keepdims=True)
        acc_ref[...] = alpha * acc_ref[...] + jnp.dot(
            vt_ref[0, j], p.astype(jnp.bfloat16), preferred_element_type=jnp.float32)
        m_ref[...] = m_new

    scores(0, 0)
    if n_chunks > 1:
        assert n_chunks % 2 == 1

        def pair(jj, carry):
            j = 2 * jj
            scores(j + 1, 1)
            softmax_pv(j, 0)
            scores(j + 2, 0)
            softmax_pv(j + 1, 1)
            return carry

        lax.fori_loop(0, n_chunks // 2, pair, 0)
    softmax_pv(n_chunks - 1, 0)

    o_t = acc_ref[...] / l_ref[...]
    if diff:
        lp = lam_ref[...]
        lam = (jnp.exp(jnp.sum(lp[0:1] * lp[1:2], axis=-1, keepdims=True))
               - jnp.exp(jnp.sum(lp[2:3] * lp[3:4], axis=-1, keepdims=True)) + lam_init)
        o_t = o_t[:, 0:rows] - lam * o_t[:, rows:2 * rows]
        o_t = o_t * lax.rsqrt(jnp.mean(o_t * o_t, axis=0, keepdims=True) + NORM_EPS)
        o_t = o_t * g_ref[...] * (1.0 - lam_init)
        o_ref[0] = (o_t.T * gz_ref[0].astype(jnp.float32)).astype(jnp.bfloat16)
    else:
        o = o_t.T
        for h in range(GQ_HEADS // GQ_KV_HEADS):
            sl = slice(h * LANES, (h + 1) * LANES)
            o_ref[0, :, sl] = (o[h * rows:(h + 1) * rows]
                               * gz_ref[0, :, sl].astype(jnp.float32)).astype(jnp.bfloat16)


def _attention(qs, k, vt, gz, lam_p, subln_g, *, diff, lam_init, n_lat, ctx_queries, name):
    bsz, n_tok, d = gz.shape
    n_ctx = n_tok - n_lat
    n_chunks_all = vt.shape[1]
    if diff:
        q_rows, q_cols, n_kv = Q_ROWS_DIFF, LANES, DA_HEADS
    else:
        q_rows, q_cols, n_kv = Q_ROWS_GQA, SOFTMAX_COLS, GQ_KV_HEADS
    if ctx_queries:
        q_off, n_q_tiles, kv_len, n_chunks = n_lat // q_rows, n_ctx // q_rows, n_ctx, 1
        kspec = pl.BlockSpec((1, n_ctx, LANES), lambda b, h, i: (b, n_lat // n_ctx, h))
        vtspec = pl.BlockSpec((1, 1, LANES, n_ctx),
                              lambda b, h, i: (b, n_chunks_all - 1, h, KV_CHUNK // n_ctx - 1))
    else:
        q_off, n_q_tiles, kv_len, n_chunks = 0, n_lat // q_rows, n_tok, n_chunks_all
        kspec = pl.BlockSpec((1, n_tok, LANES), lambda b, h, i: (b, 0, h))
        vtspec = pl.BlockSpec((1, n_chunks_all, LANES, KV_CHUNK), lambda b, h, i: (b, 0, h, 0))
    kv_chunk = kv_len // n_chunks
    qspec = pl.BlockSpec((1, q_rows, q_cols), lambda b, h, i: (b, q_off + i, h))
    in_specs = [qspec] * len(qs) + [kspec, vtspec, qspec]
    args = list(qs) + [k, vt, gz]
    if diff:
        in_specs += [pl.BlockSpec((4, DA_HEAD_DIM), lambda b, h, i: (0, 0)),
                     pl.BlockSpec((LANES, 1), lambda b, h, i: (0, 0))]
        args += [lam_p, subln_g.reshape(LANES, 1)]
    return pl.pallas_call(
        functools.partial(_attn_kernel, diff=diff, n_chunks=n_chunks, lam_init=lam_init),
        grid=(bsz, n_kv, n_q_tiles),
        in_specs=in_specs,
        out_specs=pl.BlockSpec((1, q_rows, q_cols), lambda b, h, i: (b, i, h)),
        out_shape=jax.ShapeDtypeStruct((bsz, n_q_tiles * q_rows, d), jnp.bfloat16),
        scratch_shapes=[
            pltpu.VMEM((SOFTMAX_COLS, LANES), jnp.bfloat16),
            pltpu.VMEM((2, kv_chunk, SOFTMAX_COLS), jnp.float32),
            pltpu.VMEM((1, SOFTMAX_COLS), jnp.float32),
            pltpu.VMEM((1, SOFTMAX_COLS), jnp.float32),
            pltpu.VMEM((LANES, SOFTMAX_COLS), jnp.float32),
        ],
        compiler_params=pltpu.CompilerParams(
            dimension_semantics=("arbitrary", "arbitrary", "arbitrary"),
            vmem_limit_bytes=VMEM_LIMIT),
        name=name,
    )(*args)


def _out_kernel(og_ref, w_ref, mod_ref, pg_ref, x_ref, o_ref):
    d = D_MODEL
    y = jnp.dot(og_ref[0], w_ref[...], preferred_element_type=jnp.float32)
    gate = mod_ref[0][:, 2 * d:3 * d]
    o_ref[0] = x_ref[0] + gate * (_rms(y) * pg_ref[...])


def _out_project(og, w_out_bf16, mods_i, post_g_i, xs, *, n_tok_out, n_lat_tiles):
    bsz, _, d = xs.shape
    tm = TOKEN_TILE
    tok = pl.BlockSpec((1, tm, d), lambda b, t: (b, t, 0))
    return pl.pallas_call(
        _out_kernel,
        grid=(bsz, n_tok_out // tm),
        in_specs=[
            tok,
            pl.BlockSpec((d, d), lambda b, t: (0, 0)),
            pl.BlockSpec((1, 1, 3 * d), _mod_index(n_lat_tiles, bsz)),
            pl.BlockSpec((1, d), lambda b, t: (0, 0)),
            tok,
        ],
        out_specs=tok,
        out_shape=jax.ShapeDtypeStruct((bsz, n_tok_out, d), jnp.float32),
        compiler_params=pltpu.CompilerParams(
            dimension_semantics=("arbitrary", "arbitrary"),
            vmem_limit_bytes=VMEM_LIMIT),
        name="out_proj",
    )(og, w_out_bf16, mods_i, post_g_i.reshape(1, d), xs)


def _rope_tables(n_lat, n_ctx, head_dim):
    rows = jnp.repeat(jnp.arange(n_lat // GRID_W, dtype=jnp.int32), GRID_W)
    cols = jnp.tile(jnp.arange(GRID_W, dtype=jnp.int32), n_lat // GRID_W)
    axis_dim = head_dim // 2
    freqs = ROPE_THETA ** (-jnp.arange(0, axis_dim, 2, dtype=jnp.float32) / axis_dim)
    ang = jnp.concatenate([rows.astype(jnp.float32)[:, None] * freqs,
                           cols.astype(jnp.float32)[:, None] * freqs], axis=-1)
    cos, sin = jnp.cos(ang), jnp.sin(ang)
    reps = LANES // head_dim
    cos_w = jnp.tile(jnp.concatenate([cos, cos], axis=-1), (1, reps))
    sin_w = jnp.tile(jnp.concatenate([-sin, sin], axis=-1), (1, reps))
    cos_w = jnp.concatenate([cos_w, jnp.ones((n_ctx, LANES), jnp.float32)], axis=0)
    sin_w = jnp.concatenate([sin_w, jnp.zeros((n_ctx, LANES), jnp.float32)], axis=0)
    return cos_w, sin_w


def kernel(x, c, ctx, c_ctx, ada_w, ada_b, pre_g, post_g, w_out, a_w_in, a_lambda, a_subln_g, b_w_in, b_qk_g):
    bsz, n_lat, d = x.shape
    n_ctx = ctx.shape[1]
    depth = ada_w.shape[0]
    n_tok = n_lat + n_ctx
    assert d == D_MODEL and n_ctx == CTX_LEN
    assert n_tok % KV_CHUNK == 0 and KV_CHUNK % n_ctx == 0 and n_lat % n_ctx == 0
    n_lat_tiles = n_lat // TOKEN_TILE

    cvecs = jnp.concatenate([c, c_ctx[None, :], jnp.zeros((7 - bsz, d), jnp.float32)], axis=0)
    mods = _mods(cvecs, ada_w, ada_b).reshape(depth, 8, 1, 3 * d)

    cos_a, sin_a = _rope_tables(n_lat, n_ctx, DA_HEAD_DIM)
    cos_b, sin_b = _rope_tables(n_lat, n_ctx, GQ_HEAD_DIM)
    w_out_bf = w_out.astype(jnp.bfloat16)
    a_w_bf = a_w_in.astype(jnp.bfloat16)
    b_w_bf = b_w_in.astype(jnp.bfloat16)

    xs = jnp.concatenate([x, ctx], axis=1)
    for i in range(depth):
        last = i == depth - 1
        diff = i % 2 == 0
        j = i // 2
        lam_init = _lambda_init(i)
        if diff:
            q1, q2, k, v, gz = _project(xs, mods[i], pre_g[i], a_w_bf[j], cos_a, sin_a, None,
                                        diff=True, n_lat_tiles=n_lat_tiles)
            qs, lam_p, sub_g = (q1, q2), a_lambda[j], a_subln_g[j]
        else:
            q, k, v, gz = _project(xs, mods[i], pre_g[i], b_w_bf[j], cos_b, sin_b, b_qk_g[j],
                                   diff=False, n_lat_tiles=n_lat_tiles)
            qs, lam_p, sub_g = (q,), None, None
        og = _attention(qs, k, v, gz, lam_p, sub_g, diff=diff, lam_init=lam_init, n_lat=n_lat,
                        ctx_queries=False, name="attn_diff" if diff else "attn_gqa")
        if last:
            return _out_project(og, w_out_bf[i], mods[i], post_g[i], xs,
                                n_tok_out=n_lat, n_lat_tiles=n_lat_tiles)
        ogc = _attention(qs, k, v, gz, lam_p, sub_g, diff=diff, lam_init=lam_init, n_lat=n_lat,
                         ctx_queries=True, name="attn_diff_ctx" if diff else "attn_gqa_ctx")
        og_all = jnp.concatenate([og, ogc], axis=1)
        xs = _out_project(og_all, w_out_bf[i], mods[i], post_g[i], xs,
                          n_tok_out=n_tok, n_lat_tiles=n_lat_tiles)
    return xs[:, :n_lat]
```

```python
import functools
import math

import jax
import jax.numpy as jnp
from jax import lax
from jax.experimental import pallas as pl
from jax.experimental.pallas import tpu as pltpu

D_MODEL = 1024
CTX_LEN = 256
GRID_W = 64
DA_HEAD_DIM = 64
DA_HEADS = 8
GQ_HEAD_DIM = 128
GQ_HEADS = 8
GQ_KV_HEADS = 2
ROPE_THETA = 10000.0
NORM_EPS = 1e-6

LANES = 128
TOKEN_TILE = 256
SOFTMAX_COLS = 512
Q_ROWS_DIFF = SOFTMAX_COLS // 2
Q_ROWS_GQA = SOFTMAX_COLS // (GQ_HEADS // GQ_KV_HEADS)
KV_CHUNK = 768
SOFTMAX_STRIP = 64
VMEM_LIMIT = 48 * 1024 * 1024
LOG2E = math.log2(math.e)

_NT = (((1,), (1,)), ((), ()))


def _lambda_init(layer_idx):
    return 0.8 - 0.6 * math.exp(-0.3 * layer_idx)


def _silu(x):
    return x / (1.0 + jnp.exp(-x))


def _rms(x):
    return x * lax.rsqrt(jnp.mean(x * x, axis=-1, keepdims=True) + NORM_EPS)


def _mods_kernel(cv_ref, w_ref, b_ref, o_ref):
    sc = _silu(cv_ref[...])
    o_ref[0] = jnp.dot(sc, w_ref[0], precision=lax.Precision.HIGHEST,
                       preferred_element_type=jnp.float32) + b_ref[0]


def _mods(cvecs, ada_w, ada_b):
    depth, d, d3 = ada_w.shape
    rows = cvecs.shape[0]
    tn = 1024
    return pl.pallas_call(
        _mods_kernel,
        grid=(depth, d3 // tn),
        in_specs=[
            pl.BlockSpec((rows, d), lambda i, j: (0, 0)),
            pl.BlockSpec((1, d, tn), lambda i, j: (i, 0, j)),
            pl.BlockSpec((1, 1, tn), lambda i, j: (i, 0, j)),
        ],
        out_specs=pl.BlockSpec((1, rows, tn), lambda i, j: (i, 0, j)),
        out_shape=jax.ShapeDtypeStruct((depth, rows, d3), jnp.float32),
        compiler_params=pltpu.CompilerParams(
            dimension_semantics=("arbitrary", "arbitrary"),
            vmem_limit_bytes=VMEM_LIMIT),
        name="adaln_mods",
    )(cvecs, ada_w, ada_b.reshape(depth, 1, d3))


def _rope_chunk(xc, cos, sin_signed, first_half, half):
    if 2 * half == LANES:
        partner = pltpu.roll(xc, half, 1)
    else:
        partner = jnp.where(first_half, pltpu.roll(xc, LANES - half, 1), pltpu.roll(xc, half, 1))
    return xc * cos + partner * sin_signed


def _proj_kernel(*refs, diff, q_scale):
    if diff:
        (x_ref, mod_ref, pg_ref, w_ref, cos_ref, sin_ref,
         q1_ref, q2_ref, k_ref, v_ref, gz_ref) = refs
    else:
        (x_ref, mod_ref, pg_ref, w_ref, cos_ref, sin_ref, qkg_ref,
         q_ref, k_ref, v_ref, gz_ref) = refs
    d = D_MODEL
    x = x_ref[0]
    mod = mod_ref[0]
    shift, scale = mod[:, 0:d], mod[:, d:2 * d]
    h = _rms(x) * pg_ref[...] * (1.0 + scale) + shift
    hb = h.astype(jnp.bfloat16)

    cos = cos_ref[...]
    sin = sin_ref[...]
    lane = lax.broadcasted_iota(jnp.int32, (1, LANES), 1)
    if diff:
        half = DA_HEAD_DIM // 2
        first_half = (lane % DA_HEAD_DIM) < half
        lo = lane < DA_HEAD_DIM
        nq, nk, nv = DA_HEADS, DA_HEADS, DA_HEADS
    else:
        half = GQ_HEAD_DIM // 2
        first_half = None
        nq, nk, nv = GQ_HEADS, GQ_KV_HEADS, GQ_KV_HEADS
    col = 0

    pq = jnp.dot(hb, w_ref[:, col:col + nq * LANES], preferred_element_type=jnp.float32)
    col += nq * LANES
    for c in range(nq):
        xc = pq[:, c * LANES:(c + 1) * LANES]
        if not diff:
            xc = _rms(xc) * qkg_ref[0:1, :]
        r = _rope_chunk(xc, cos, sin, first_half, half) * q_scale
        sl = slice(c * LANES, (c + 1) * LANES)
        if diff:
            q1_ref[0, :, sl] = jnp.where(lo, r, 0.0).astype(jnp.bfloat16)
            q2_ref[0, :, sl] = jnp.where(lo, 0.0, r).astype(jnp.bfloat16)
        else:
            q_ref[0, :, sl] = r.astype(jnp.bfloat16)

    pk = jnp.dot(hb, w_ref[:, col:col + nk * LANES], preferred_element_type=jnp.float32)
    col += nk * LANES
    for c in range(nk):
        xc = pk[:, c * LANES:(c + 1) * LANES]
        if not diff:
            xc = _rms(xc) * qkg_ref[1:2, :]
        r = _rope_chunk(xc, cos, sin, first_half, half)
        k_ref[0, :, c * LANES:(c + 1) * LANES] = r.astype(jnp.bfloat16)

    pv = jnp.dot(hb, w_ref[:, col:col + nv * LANES], preferred_element_type=jnp.float32)
    col += nv * LANES
    v_ref[0, 0] = pv.T.astype(jnp.bfloat16)

    pz = jnp.dot(hb, w_ref[:, col:col + d], preferred_element_type=jnp.float32)
    gz_ref[0] = _silu(pz).astype(jnp.bfloat16)


def _mod_index(n_lat_tiles, n_batch):
    def index(b, t):
        return (jnp.where(t < n_lat_tiles, b, n_batch), 0, 0)
    return index


def _project(xs, mods_i, pre_g_i, w_in_bf16, cos, sin, qk_g, *, diff, n_lat_tiles):
    bsz, n_tok, d = xs.shape
    tm = TOKEN_TILE
    width = w_in_bf16.shape[1]
    q_scale = (DA_HEAD_DIM if diff else GQ_HEAD_DIM) ** -0.5 * LOG2E
    tok = lambda w: pl.BlockSpec((1, tm, w), lambda b, t: (b, t, 0))
    in_specs = [
        tok(d),
        pl.BlockSpec((1, 1, 3 * d), _mod_index(n_lat_tiles, bsz)),
        pl.BlockSpec((1, d), lambda b, t: (0, 0)),
        pl.BlockSpec((d, width), lambda b, t: (0, 0)),
        pl.BlockSpec((tm, LANES), lambda b, t: (t, 0)),
        pl.BlockSpec((tm, LANES), lambda b, t: (t, 0)),
    ]
    args = [xs, mods_i, pre_g_i.reshape(1, d), w_in_bf16, cos, sin]
    bf = lambda w: jax.ShapeDtypeStruct((bsz, n_tok, w), jnp.bfloat16)
    kvw = (DA_HEADS if diff else GQ_KV_HEADS) * LANES
    tiles_per_chunk = KV_CHUNK // tm
    vt_shape = jax.ShapeDtypeStruct((bsz, n_tok // KV_CHUNK, kvw, KV_CHUNK), jnp.bfloat16)
    vt_spec = pl.BlockSpec((1, 1, kvw, tm),
                           lambda b, t: (b, t // tiles_per_chunk, 0, t % tiles_per_chunk))
    if diff:
        out_shape = [bf(d), bf(d), bf(kvw), vt_shape, bf(d)]
        out_specs = [tok(d), tok(d), tok(kvw), vt_spec, tok(d)]
    else:
        in_specs.append(pl.BlockSpec((2, LANES), lambda b, t: (0, 0)))
        args.append(qk_g)
        out_shape = [bf(d), bf(kvw), vt_shape, bf(d)]
        out_specs = [tok(d), tok(kvw), vt_spec, tok(d)]
    return pl.pallas_call(
        functools.partial(_proj_kernel, diff=diff, q_scale=q_scale),
        grid=(bsz, n_tok // tm),
        in_specs=in_specs,
        out_specs=out_specs,
        out_shape=out_shape,
        compiler_params=pltpu.CompilerParams(
            dimension_semantics=("arbitrary", "arbitrary"),
            vmem_limit_bytes=VMEM_LIMIT),
        name="proj_diff" if diff else "proj_gqa",
    )(*args)


def _attn_kernel(*refs, diff, n_chunks, lam_init):
    if diff:
        (q1_ref, q2_ref, k_ref, vt_ref, gz_ref, lam_ref, g_ref,
         o_ref, qs_ref, s_ref, mc_ref, p_ref, m_ref, l_ref, al_ref, acc_ref) = refs
        rows = q1_ref.shape[1]
        qs_ref[0:rows] = q1_ref[0]
        qs_ref[rows:2 * rows] = q2_ref[0]
    else:
        (q_ref, k_ref, vt_ref, gz_ref,
         o_ref, qs_ref, s_ref, mc_ref, p_ref, m_ref, l_ref, al_ref, acc_ref) = refs
        rows = q_ref.shape[1]
        for h in range(GQ_HEADS // GQ_KV_HEADS):
            qs_ref[h * rows:(h + 1) * rows] = q_ref[0, :, h * LANES:(h + 1) * LANES]
    kv_chunk = vt_ref.shape[3]

    m_ref[...] = jnp.full(m_ref.shape, -jnp.inf, jnp.float32)
    l_ref[...] = jnp.zeros(l_ref.shape, jnp.float32)
    acc_ref[...] = jnp.zeros(acc_ref.shape, jnp.float32)

    cols = SOFTMAX_COLS
    sub = 8
    strip = SOFTMAX_STRIP

    def scores(j, slot):
        off = j * kv_chunk
        if not isinstance(off, int):
            off = pl.multiple_of(off, kv_chunk)
        s = lax.dot_general(k_ref[0, pl.ds(off, kv_chunk), :], qs_ref[...], _NT,
                            preferred_element_type=jnp.float32)
        s3 = s.reshape(kv_chunk // sub, sub, cols)
        s_ref[slot] = s3
        mc_ref[slot] = jnp.max(jnp.max(s3, axis=0), axis=0, keepdims=True)

    def softmax(slot):
        m_old = m_ref[...]
        m_new = jnp.maximum(m_old, mc_ref[slot])
        alpha = jnp.exp2(m_old - m_new)
        m8 = jnp.broadcast_to(m_new, (sub, cols))
        lpart = jnp.zeros((sub, cols), jnp.float32)
        for r in range(0, kv_chunk, strip):
            p3 = jnp.exp2(s_ref[slot, r // sub:(r + strip) // sub] - m8)
            lpart = lpart + jnp.sum(p3, axis=0)
            p_ref[slot, r:r + strip, :] = p3.reshape(strip, cols).astype(jnp.bfloat16)
        l_ref[...] = alpha * l_ref[...] + jnp.sum(lpart, axis=0, keepdims=True)
        m_ref[...] = m_new
        al_ref[...] = alpha

    def pv(j, slot):
        acc_ref[...] = al_ref[...] * acc_ref[...] + jnp.dot(
            vt_ref[0, j], p_ref[slot], preferred_element_type=jnp.float32)

    n = n_chunks
    scores(0, 0)
    if n > 1:
        assert n % 2 == 1
        scores(1, 1)
    softmax(0)
    if n > 1:
        def pair(jj, carry):
            t = 2 * jj + 1
            pv(t - 1, 0)
            scores(t + 1, 0)
            softmax(1)
            pv(t, 1)
            scores(t + 2, 1)
            softmax(0)
            return carry

        lax.fori_loop(0, (n - 3) // 2, pair, 0)
        pv(n - 3, 0)
        scores(n - 1, 0)
        softmax(1)
        pv(n - 2, 1)
        softmax(0)
    pv(n - 1, 0)

    o_t = acc_ref[...] / l_ref[...]
    if diff:
        lp = lam_ref[...]
        lam = (jnp.exp(jnp.sum(lp[0:1] * lp[1:2], axis=-1, keepdims=True))
               - jnp.exp(jnp.sum(lp[2:3] * lp[3:4], axis=-1, keepdims=True)) + lam_init)
        o_t = o_t[:, 0:rows] - lam * o_t[:, rows:2 * rows]
        o_t = o_t * lax.rsqrt(jnp.mean(o_t * o_t, axis=0, keepdims=True) + NORM_EPS)
        o_t = o_t * g_ref[...] * (1.0 - lam_init)
        o_ref[0] = (o_t.T * gz_ref[0].astype(jnp.float32)).astype(jnp.bfloat16)
    else:
        o = o_t.T
        for h in range(GQ_HEADS // GQ_KV_HEADS):
            sl = slice(h * LANES, (h + 1) * LANES)
            o_ref[0, :, sl] = (o[h * rows:(h + 1) * rows]
                               * gz_ref[0, :, sl].astype(jnp.float32)).astype(jnp.bfloat16)


def _attention(qs, k, vt, gz, lam_p, subln_g, *, diff, lam_init, n_lat, ctx_queries, name):
    bsz, n_tok, d = gz.shape
    n_ctx = n_tok - n_lat
    n_chunks_all = vt.shape[1]
    if diff:
        q_rows, q_cols, n_kv = Q_ROWS_DIFF, LANES, DA_HEADS
    else:
        q_rows, q_cols, n_kv = Q_ROWS_GQA, SOFTMAX_COLS, GQ_KV_HEADS
    if ctx_queries:
        q_off, n_q_tiles, kv_len, n_chunks = n_lat // q_rows, n_ctx // q_rows, n_ctx, 1
        kspec = pl.BlockSpec((1, n_ctx, LANES), lambda b, h, i: (b, n_lat // n_ctx, h))
        vtspec = pl.BlockSpec((1, 1, LANES, n_ctx),
                              lambda b, h, i: (b, n_chunks_all - 1, h, KV_CHUNK // n_ctx - 1))
    else:
        q_off, n_q_tiles, kv_len, n_chunks = 0, n_lat // q_rows, n_tok, n_chunks_all
        kspec = pl.BlockSpec((1, n_tok, LANES), lambda b, h, i: (b, 0, h))
        vtspec = pl.BlockSpec((1, n_chunks_all, LANES, KV_CHUNK), lambda b, h, i: (b, 0, h, 0))
    kv_chunk = kv_len // n_chunks
    qspec = pl.BlockSpec((1, q_rows, q_cols), lambda b, h, i: (b, q_off + i, h))
    in_specs = [qspec] * len(qs) + [kspec, vtspec, qspec]
    args = list(qs) + [k, vt, gz]
    if diff:
        in_specs += [pl.BlockSpec((4, DA_HEAD_DIM), lambda b, h, i: (0, 0)),
                     pl.BlockSpec((LANES, 1), lambda b, h, i: (0, 0))]
        args += [lam_p, subln_g.reshape(LANES, 1)]
    return pl.pallas_call(
        functools.partial(_attn_kernel, diff=diff, n_chunks=n_chunks, lam_init=lam_init),
        grid=(bsz, n_kv, n_q_tiles),
        in_specs=in_specs,
        out_specs=pl.BlockSpec((1, q_rows, q_cols), lambda b, h, i: (b, i, h)),
        out_shape=jax.ShapeDtypeStruct((bsz, n_q_tiles * q_rows, d), jnp.bfloat16),
        scratch_shapes=[
            pltpu.VMEM((SOFTMAX_COLS, LANES), jnp.bfloat16),
            pltpu.VMEM((2, kv_chunk // 8, 8, SOFTMAX_COLS), jnp.float32),
            pltpu.VMEM((2, 1, SOFTMAX_COLS), jnp.float32),
            pltpu.VMEM((2, kv_chunk, SOFTMAX_COLS), jnp.bfloat16),
            pltpu.VMEM((1, SOFTMAX_COLS), jnp.float32),
            pltpu.VMEM((1, SOFTMAX_COLS), jnp.float32),
            pltpu.VMEM((1, SOFTMAX_COLS), jnp.float32),
            pltpu.VMEM((LANES, SOFTMAX_COLS), jnp.float32),
        ],
        compiler_params=pltpu.CompilerParams(
            dimension_semantics=("arbitrary", "arbitrary", "arbitrary"),
            vmem_limit_bytes=VMEM_LIMIT),
        name=name,
    )(*args)


def _out_kernel(og_ref, w_ref, mod_ref, pg_ref, x_ref, o_ref):
    d = D_MODEL
    y = jnp.dot(og_ref[0], w_ref[...], preferred_element_type=jnp.float32)
    gate = mod_ref[0][:, 2 * d:3 * d]
    o_ref[0] = x_ref[0] + gate * (_rms(y) * pg_ref[...])


def _out_project(og, w_out_bf16, mods_i, post_g_i, xs, *, n_tok_out, n_lat_tiles):
    bsz, _, d = xs.shape
    tm = TOKEN_TILE
    tok = pl.BlockSpec((1, tm, d), lambda b, t: (b, t, 0))
    return pl.pallas_call(
        _out_kernel,
        grid=(bsz, n_tok_out // tm),
        in_specs=[
            tok,
            pl.BlockSpec((d, d), lambda b, t: (0, 0)),
            pl.BlockSpec((1, 1, 3 * d), _mod_index(n_lat_tiles, bsz)),
            pl.BlockSpec((1, d), lambda b, t: (0, 0)),
            tok,
        ],
        out_specs=tok,
        out_shape=jax.ShapeDtypeStruct((bsz, n_tok_out, d), jnp.float32),
        compiler_params=pltpu.CompilerParams(
            dimension_semantics=("arbitrary", "arbitrary"),
            vmem_limit_bytes=VMEM_LIMIT),
        name="out_proj",
    )(og, w_out_bf16, mods_i, post_g_i.reshape(1, d), xs)


def _rope_tables(n_lat, n_ctx, head_dim):
    rows = jnp.repeat(jnp.arange(n_lat // GRID_W, dtype=jnp.int32), GRID_W)
    cols = jnp.tile(jnp.arange(GRID_W, dtype=jnp.int32), n_lat // GRID_W)
    axis_dim = head_dim // 2
    freqs = ROPE_THETA ** (-jnp.arange(0, axis_dim, 2, dtype=jnp.float32) / axis_dim)
    ang = jnp.concatenate([rows.astype(jnp.float32)[:, None] * freqs,
                           cols.astype(jnp.float32)[:, None] * freqs], axis=-1)
    cos, sin = jnp.cos(ang), jnp.sin(ang)
    reps = LANES // head_dim
    cos_w = jnp.tile(jnp.concatenate([cos, cos], axis=-1), (1, reps))
    sin_w = jnp.tile(jnp.concatenate([-sin, sin], axis=-1), (1, reps))
    cos_w = jnp.concatenate([cos_w, jnp.ones((n_ctx, LANES), jnp.float32)], axis=0)
    sin_w = jnp.concatenate([sin_w, jnp.zeros((n_ctx, LANES), jnp.float32)], axis=0)
    return cos_w, sin_w


def kernel(x, c, ctx, c_ctx, ada_w, ada_b, pre_g, post_g, w_out, a_w_in, a_lambda, a_subln_g, b_w_in, b_qk_g):
    bsz, n_lat, d = x.shape
    n_ctx = ctx.shape[1]
    depth = ada_w.shape[0]
    n_tok = n_lat + n_ctx
    assert d == D_MODEL and n_ctx == CTX_LEN
    assert n_tok % KV_CHUNK == 0 and KV_CHUNK % n_ctx == 0 and n_lat % n_ctx == 0
    n_lat_tiles = n_lat // TOKEN_TILE

    cvecs = jnp.concatenate([c, c_ctx[None, :], jnp.zeros((7 - bsz, d), jnp.float32)], axis=0)
    mods = _mods(cvecs, ada_w, ada_b).reshape(depth, 8, 1, 3 * d)

    cos_a, sin_a = _rope_tables(n_lat, n_ctx, DA_HEAD_DIM)
    cos_b, sin_b = _rope_tables(n_lat, n_ctx, GQ_HEAD_DIM)
    w_out_bf = w_out.astype(jnp.bfloat16)
    a_w_bf = a_w_in.astype(jnp.bfloat16)
    b_w_bf = b_w_in.astype(jnp.bfloat16)

    xs = jnp.concatenate([x, ctx], axis=1)
    for i in range(depth):
        last = i == depth - 1
        diff = i % 2 == 0
        j = i // 2
        lam_init = _lambda_init(i)
        if diff:
            q1, q2, k, v, gz = _project(xs, mods[i], pre_g[i], a_w_bf[j], cos_a, sin_a, None,
                                        diff=True, n_lat_tiles=n_lat_tiles)
            qs, lam_p, sub_g = (q1, q2), a_lambda[j], a_subln_g[j]
        else:
            q, k, v, gz = _project(xs, mods[i], pre_g[i], b_w_bf[j], cos_b, sin_b, b_qk_g[j],
                                   diff=False, n_lat_tiles=n_lat_tiles)
            qs, lam_p, sub_g = (q,), None, None
        og = _attention(qs, k, v, gz, lam_p, sub_g, diff=diff, lam_init=lam_init, n_lat=n_lat,
                        ctx_queries=False, name="attn_diff" if diff else "attn_gqa")
        if last:
            return _out_project(og, w_out_bf[i], mods[i], post_g[i], xs,
                                n_tok_out=n_lat, n_lat_tiles=n_lat_tiles)
        ogc = _attention(qs, k, v, gz, lam_p, sub_g, diff=diff, lam_init=lam_init, n_lat=n_lat,
                         ctx_queries=True, name="attn_diff_ctx" if diff else "attn_gqa_ctx")
        og_all = jnp.concatenate([og, ogc], axis=1)
        xs = _out_project(og_all, w_out_bf[i], mods[i], post_g[i], xs,
                          n_tok_out=n_tok, n_lat_tiles=n_lat_tiles)
    return xs[:, :n_lat]
```

```python
import functools
import math

import jax
import jax.numpy as jnp
from jax import lax
from jax.experimental import pallas as pl
from jax.experimental.pallas import tpu as pltpu

D_MODEL = 1024
CTX_LEN = 256
GRID_W = 64
DA_HEAD_DIM = 64
DA_HEADS = 8
GQ_HEAD_DIM = 128
GQ_HEADS = 8
GQ_KV_HEADS = 2
GQ_GROUP = GQ_HEADS // GQ_KV_HEADS
ROPE_THETA = 10000.0
NORM_EPS = 1e-6

LANES = 128
TOKEN_TILE = 256
SOFTMAX_COLS = 512
Q_ROWS_DIFF = SOFTMAX_COLS // 2
Q_ROWS_GQA = SOFTMAX_COLS // GQ_GROUP
KV_CHUNKS = 6
TILES_PER_STEP_DIFF = 32
TILES_PER_STEP_GQA = 16
VMEM_LIMIT = 48 * 1024 * 1024
LOG2E = math.log2(math.e)

_NT = (((1,), (1,)), ((), ()))


def _lambda_init(layer_idx):
    return 0.8 - 0.6 * math.exp(-0.3 * layer_idx)


def _silu(x):
    return x / (1.0 + jnp.exp(-x))


def _rms(x):
    return x * lax.rsqrt(jnp.mean(x * x, axis=-1, keepdims=True) + NORM_EPS)


def _mods_kernel(cv_ref, w_ref, b_ref, o_ref):
    sc = _silu(cv_ref[...])
    o_ref[0] = jnp.dot(sc, w_ref[0], precision=lax.Precision.HIGHEST,
                       preferred_element_type=jnp.float32) + b_ref[0]


def _mods(cvecs, ada_w, ada_b):
    depth, d, d3 = ada_w.shape
    rows = cvecs.shape[0]
    tn = 1024
    return pl.pallas_call(
        _mods_kernel,
        grid=(depth, d3 // tn),
        in_specs=[
            pl.BlockSpec((rows, d), lambda i, j: (0, 0)),
            pl.BlockSpec((1, d, tn), lambda i, j: (i, 0, j)),
            pl.BlockSpec((1, 1, tn), lambda i, j: (i, 0, j)),
        ],
        out_specs=pl.BlockSpec((1, rows, tn), lambda i, j: (i, 0, j)),
        out_shape=jax.ShapeDtypeStruct((depth, rows, d3), jnp.float32),
        compiler_params=pltpu.CompilerParams(
            dimension_semantics=("arbitrary", "arbitrary"),
            vmem_limit_bytes=VMEM_LIMIT),
        name="adaln_mods",
    )(cvecs, ada_w, ada_b.reshape(depth, 1, d3))


def _rope_chunk(xc, cos, sin_signed, first_half, half):
    if 2 * half == LANES:
        partner = pltpu.roll(xc, half, 1)
    else:
        partner = jnp.where(first_half, pltpu.roll(xc, LANES - half, 1), pltpu.roll(xc, half, 1))
    return xc * cos + partner * sin_signed


def _proj_kernel(*refs, diff, q_scale):
    if diff:
        (x_ref, mod_ref, pg_ref, w_ref, cos_ref, sin_ref,
         q_ref, k_ref, v_ref, gz_ref) = refs
    else:
        (x_ref, mod_ref, pg_ref, w_ref, cos_ref, sin_ref, qkg_ref,
         q_ref, k_ref, v_ref, gz_ref) = refs
    d = D_MODEL
    tm = x_ref.shape[1]
    x = x_ref[0]
    mod = mod_ref[0]
    shift, scale = mod[:, 0:d], mod[:, d:2 * d]
    h = _rms(x) * pg_ref[...] * (1.0 + scale) + shift
    hb = h.astype(jnp.bfloat16)

    cos = cos_ref[...]
    sin = sin_ref[...]
    lane = lax.broadcasted_iota(jnp.int32, (1, LANES), 1)
    if diff:
        half = DA_HEAD_DIM // 2
        first_half = (lane % DA_HEAD_DIM) < half
        lo = lane < DA_HEAD_DIM
        nq, nk, nv = DA_HEADS, DA_HEADS, DA_HEADS
    else:
        half = GQ_HEAD_DIM // 2
        first_half = None
        nq, nk, nv = GQ_HEADS, GQ_KV_HEADS, GQ_KV_HEADS
    col = 0

    pq = jnp.dot(hb, w_ref[:, col:col + nq * LANES], preferred_element_type=jnp.float32)
    col += nq * LANES
    for c in range(nq):
        xc = pq[:, c * LANES:(c + 1) * LANES]
        if not diff:
            xc = _rms(xc) * qkg_ref[0:1, :]
        r = _rope_chunk(xc, cos, sin, first_half, half) * q_scale
        if diff:
            sl = slice(c * LANES, (c + 1) * LANES)
            q_ref[0, 0, 0:tm, sl] = jnp.where(lo, r, 0.0).astype(jnp.bfloat16)
            q_ref[0, 0, tm:2 * tm, sl] = jnp.where(lo, 0.0, r).astype(jnp.bfloat16)
        else:
            rb = r.astype(jnp.bfloat16)
            g, hh = divmod(c, GQ_GROUP)
            for ii in range(tm // Q_ROWS_GQA):
                q_ref[0, ii, g, hh * Q_ROWS_GQA:(hh + 1) * Q_ROWS_GQA, :] = (
                    rb[ii * Q_ROWS_GQA:(ii + 1) * Q_ROWS_GQA])

    pk = jnp.dot(hb, w_ref[:, col:col + nk * LANES], preferred_element_type=jnp.float32)
    col += nk * LANES
    for c in range(nk):
        xc = pk[:, c * LANES:(c + 1) * LANES]
        if not diff:
            xc = _rms(xc) * qkg_ref[1:2, :]
        r = _rope_chunk(xc, cos, sin, first_half, half)
        k_ref[0, :, c * LANES:(c + 1) * LANES] = r.astype(jnp.bfloat16)

    pv = jnp.dot(hb, w_ref[:, col:col + nv * LANES], preferred_element_type=jnp.float32)
    col += nv * LANES
    v_ref[0] = pv.astype(jnp.bfloat16)

    pz = jnp.dot(hb, w_ref[:, col:col + d], preferred_element_type=jnp.float32)
    gz_ref[0] = _silu(pz).astype(jnp.bfloat16)


def _mod_index(n_lat_tiles, n_batch):
    def index(b, t):
        return (jnp.where(t < n_lat_tiles, b, n_batch), 0, 0)
    return index


def _project(xs, mods_i, pre_g_i, w_in_bf16, cos, sin, qk_g, *, diff, n_lat_tiles):
    bsz, n_tok, d = xs.shape
    tm = TOKEN_TILE
    assert tm == Q_ROWS_DIFF and tm % Q_ROWS_GQA == 0
    width = w_in_bf16.shape[1]
    q_scale = (DA_HEAD_DIM if diff else GQ_HEAD_DIM) ** -0.5 * LOG2E
    tok = lambda w: pl.BlockSpec((1, tm, w), lambda b, t: (b, t, 0))
    in_specs = [
        tok(d),
        pl.BlockSpec((1, 1, 3 * d), _mod_index(n_lat_tiles, bsz)),
        pl.BlockSpec((1, d), lambda b, t: (0, 0)),
        pl.BlockSpec((d, width), lambda b, t: (0, 0)),
        pl.BlockSpec((tm, LANES), lambda b, t: (t, 0)),
        pl.BlockSpec((tm, LANES), lambda b, t: (t, 0)),
    ]
    args = [xs, mods_i, pre_g_i.reshape(1, d), w_in_bf16, cos, sin]
    bf = lambda w: jax.ShapeDtypeStruct((bsz, n_tok, w), jnp.bfloat16)
    kvw = (DA_HEADS if diff else GQ_KV_HEADS) * LANES
    if diff:
        q_shape = jax.ShapeDtypeStruct((bsz, n_tok // tm, SOFTMAX_COLS, d), jnp.bfloat16)
        q_spec = pl.BlockSpec((1, 1, SOFTMAX_COLS, d), lambda b, t: (b, t, 0, 0))
    else:
        per = tm // Q_ROWS_GQA
        in_specs.append(pl.BlockSpec((2, LANES), lambda b, t: (0, 0)))
        args.append(qk_g)
        q_shape = jax.ShapeDtypeStruct(
            (bsz, n_tok // Q_ROWS_GQA, GQ_KV_HEADS, SOFTMAX_COLS, LANES), jnp.bfloat16)
        q_spec = pl.BlockSpec((1, per, GQ_KV_HEADS, SOFTMAX_COLS, LANES),
                              lambda b, t: (b, t, 0, 0, 0))
    return pl.pallas_call(
        functools.partial(_proj_kernel, diff=diff, q_scale=q_scale),
        grid=(bsz, n_tok // tm),
        in_specs=in_specs,
        out_specs=[q_spec, tok(kvw), tok(kvw), tok(d)],
        out_shape=[q_shape, bf(kvw), bf(kvw), bf(d)],
        compiler_params=pltpu.CompilerParams(
            dimension_semantics=("arbitrary", "arbitrary"),
            vmem_limit_bytes=VMEM_LIMIT),
        name="proj_diff" if diff else "proj_gqa",
    )(*args)


def _attn_kernel(*refs, diff, n_tiles, n_chunks, lam_init):
    if diff:
        (q_ref, k_ref, v_ref, gz_ref, lam_ref, g_ref,
         o_ref, vt_ref, s_ref, m_ref, l_ref, acc_ref) = refs
        rows = Q_ROWS_DIFF
        q_tile = lambda i: q_ref[0, i]
    else:
        (q_ref, k_ref, v_ref, gz_ref,
         o_ref, vt_ref, s_ref, m_ref, l_ref, acc_ref) = refs
        rows = Q_ROWS_GQA
        q_tile = lambda i: q_ref[0, i, 0]
    kv_chunk = vt_ref.shape[2]

    @pl.when(pl.program_id(2) == 0)
    def _transpose_values():
        for c in range(n_chunks):
            vc = v_ref[0, c * kv_chunk:(c + 1) * kv_chunk, :].astype(jnp.float32)
            vt_ref[c] = vc.T.astype(jnp.bfloat16)

    def reset():
        m_ref[...] = jnp.full(m_ref.shape, -jnp.inf, jnp.float32)
        l_ref[...] = jnp.zeros(l_ref.shape, jnp.float32)
        acc_ref[...] = jnp.zeros(acc_ref.shape, jnp.float32)

    def scores(i, c, slot):
        off = c * kv_chunk
        if not isinstance(off, int):
            off = pl.multiple_of(off, 16)
        s_ref[slot] = lax.dot_general(k_ref[0, pl.ds(off, kv_chunk), :], q_tile(i), _NT,
                                      preferred_element_type=jnp.float32)

    def softmax_pv(c, slot):
        s = s_ref[slot]
        m_old = m_ref[...]
        m_new = jnp.maximum(m_old, jnp.max(s, axis=0, keepdims=True))
        alpha = jnp.exp2(m_old - m_new)
        p = jnp.exp2(s - m_new)
        l_ref[...] = alpha * l_ref[...] + jnp.sum(p, axis=0, keepdims=True)
        acc_ref[...] = alpha * acc_ref[...] + jnp.dot(
            vt_ref[c], p.astype(jnp.bfloat16), preferred_element_type=jnp.float32)
        m_ref[...] = m_new

    def finalize(i):
        r0 = i * rows
        if not isinstance(r0, int):
            r0 = pl.multiple_of(r0, rows)
        o_t = acc_ref[...] / l_ref[...]
        if diff:
            lp = lam_ref[...]
            lam = (jnp.exp(jnp.sum(lp[0:1] * lp[1:2], axis=-1, keepdims=True))
                   - jnp.exp(jnp.sum(lp[2:3] * lp[3:4], axis=-1, keepdims=True)) + lam_init)
            o_t = o_t[:, 0:rows] - lam * o_t[:, rows:2 * rows]
            o_t = o_t * lax.rsqrt(jnp.mean(o_t * o_t, axis=0, keepdims=True) + NORM_EPS)
            o_t = o_t * g_ref[...] * (1.0 - lam_init)
            gate = gz_ref[0, pl.ds(r0, rows), :].astype(jnp.float32)
            o_ref[0, pl.ds(r0, rows), :] = (o_t.T * gate).astype(jnp.bfloat16)
        else:
            o = o_t.T
            for h in range(GQ_GROUP):
                sl = slice(h * LANES, (h + 1) * LANES)
                gate = gz_ref[0, pl.ds(r0, rows), sl].astype(jnp.float32)
                o_ref[0, pl.ds(r0, rows), sl] = (o[h * rows:(h + 1) * rows] * gate).astype(jnp.bfloat16)

    reset()
    scores(0, 0, 0)
    if n_chunks == 1:
        assert n_tiles == 1
        softmax_pv(0, 0)
        finalize(0)
        return

    assert n_chunks % 2 == 0
    pairs = n_chunks // 2

    def pair(_, carry):
        i, pi = carry
        c0 = 2 * pi
        last = pi == pairs - 1
        scores(i, c0 + 1, 1)
        softmax_pv(c0, 0)
        i_next = jnp.where(last, jnp.minimum(i + 1, n_tiles - 1), i)
        c_next = jnp.where(last, 0, c0 + 2)
        scores(i_next, c_next, 0)
        softmax_pv(c0 + 1, 1)

        @pl.when(last)
        def _():
            finalize(i)
            reset()

        return (jnp.where(last, i + 1, i), jnp.where(last, 0, pi + 1))

    lax.fori_loop(0, n_tiles * pairs, pair, (jnp.int32(0), jnp.int32(0)))


def _attention(qz, k, v, gz, lam_p, subln_g, *, diff, lam_init, n_lat, ctx_queries, name):
    bsz, n_tok, d = gz.shape
    n_ctx = n_tok - n_lat
    if diff:
        q_rows, o_cols, n_kv, tiles = Q_ROWS_DIFF, LANES, DA_HEADS, TILES_PER_STEP_DIFF
    else:
        q_rows, o_cols, n_kv, tiles = Q_ROWS_GQA, SOFTMAX_COLS, GQ_KV_HEADS, TILES_PER_STEP_GQA
    if ctx_queries:
        tiles, n_chunks, kv_len = 1, 1, n_ctx
        tile_off, n_steps = n_lat // q_rows, n_ctx // q_rows
        kv_block = n_lat // n_ctx
    else:
        n_chunks, kv_len = KV_CHUNKS, n_tok
        tiles = min(tiles, n_lat // q_rows)
        assert n_lat % (tiles * q_rows) == 0 and n_tok % (KV_CHUNKS * LANES) == 0
        tile_off, n_steps = 0, n_lat // (tiles * q_rows)
        kv_block = 0
    kv_chunk = kv_len // n_chunks
    if diff:
        qspec = pl.BlockSpec((1, tiles, SOFTMAX_COLS, LANES),
                             lambda b, h, t: (b, tile_off + t, 0, h))
    else:
        qspec = pl.BlockSpec((1, tiles, 1, SOFTMAX_COLS, LANES),
                             lambda b, h, t: (b, tile_off + t, h, 0, 0))
    kvspec = pl.BlockSpec((1, kv_len, LANES), lambda b, h, t: (b, kv_block, h))
    gzspec = pl.BlockSpec((1, tiles * q_rows, o_cols), lambda b, h, t: (b, tile_off + t, h))
    in_specs = [qspec, kvspec, kvspec, gzspec]
    args = [qz, k, v, gz]
    if diff:
        in_specs += [pl.BlockSpec((4, DA_HEAD_DIM), lambda b, h, t: (0, 0)),
                     pl.BlockSpec((LANES, 1), lambda b, h, t: (0, 0))]
        args += [lam_p, subln_g.reshape(LANES, 1)]
    return pl.pallas_call(
        functools.partial(_attn_kernel, diff=diff, n_tiles=tiles, n_chunks=n_chunks,
                          lam_init=lam_init),
        grid=(bsz, n_kv, n_steps),
        in_specs=in_specs,
        out_specs=pl.BlockSpec((1, tiles * q_rows, o_cols), lambda b, h, t: (b, t, h)),
        out_shape=jax.ShapeDtypeStruct((bsz, n_steps * tiles * q_rows, d), jnp.bfloat16),
        scratch_shapes=[
            pltpu.VMEM((n_chunks, LANES, kv_chunk), jnp.bfloat16),
            pltpu.VMEM((2, kv_chunk, SOFTMAX_COLS), jnp.float32),
            pltpu.VMEM((1, SOFTMAX_COLS), jnp.float32),
            pltpu.VMEM((1, SOFTMAX_COLS), jnp.float32),
            pltpu.VMEM((LANES, SOFTMAX_COLS), jnp.float32),
        ],
        compiler_params=pltpu.CompilerParams(
            dimension_semantics=("arbitrary", "arbitrary", "arbitrary"),
            vmem_limit_bytes=VMEM_LIMIT),
        name=name,
    )(*args)


def _out_kernel(og_ref, w_ref, mod_ref, pg_ref, x_ref, o_ref):
    d = D_MODEL
    y = jnp.dot(og_ref[0], w_ref[...], preferred_element_type=jnp.float32)
    gate = mod_ref[0][:, 2 * d:3 * d]
    o_ref[0] = x_ref[0] + gate * (_rms(y) * pg_ref[...])


def _out_project(og, w_out_bf16, mods_i, post_g_i, xs, *, n_tok_out, n_lat_tiles):
    bsz, _, d = xs.shape
    tm = TOKEN_TILE
    tok = pl.BlockSpec((1, tm, d), lambda b, t: (b, t, 0))
    return pl.pallas_call(
        _out_kernel,
        grid=(bsz, n_tok_out // tm),
        in_specs=[
            tok,
            pl.BlockSpec((d, d), lambda b, t: (0, 0)),
            pl.BlockSpec((1, 1, 3 * d), _mod_index(n_lat_tiles, bsz)),
            pl.BlockSpec((1, d), lambda b, t: (0, 0)),
            tok,
        ],
        out_specs=tok,
        out_shape=jax.ShapeDtypeStruct((bsz, n_tok_out, d), jnp.float32),
        compiler_params=pltpu.CompilerParams(
            dimension_semantics=("arbitrary", "arbitrary"),
            vmem_limit_bytes=VMEM_LIMIT),
        name="out_proj",
    )(og, w_out_bf16, mods_i, post_g_i.reshape(1, d), xs)


def _rope_tables(n_lat, n_ctx, head_dim):
    rows = jnp.repeat(jnp.arange(n_lat // GRID_W, dtype=jnp.int32), GRID_W)
    cols = jnp.tile(jnp.arange(GRID_W, dtype=jnp.int32), n_lat // GRID_W)
    axis_dim = head_dim // 2
    freqs = ROPE_THETA ** (-jnp.arange(0, axis_dim, 2, dtype=jnp.float32) / axis_dim)
    ang = jnp.concatenate([rows.astype(jnp.float32)[:, None] * freqs,
                           cols.astype(jnp.float32)[:, None] * freqs], axis=-1)
    cos, sin = jnp.cos(ang), jnp.sin(ang)
    reps = LANES // head_dim
    cos_w = jnp.tile(jnp.concatenate([cos, cos], axis=-1), (1, reps))
    sin_w = jnp.tile(jnp.concatenate([-sin, sin], axis=-1), (1, reps))
    cos_w = jnp.concatenate([cos_w, jnp.ones((n_ctx, LANES), jnp.float32)], axis=0)
    sin_w = jnp.concatenate([sin_w, jnp.zeros((n_ctx, LANES), jnp.float32)], axis=0)
    return cos_w, sin_w


def kernel(x, c, ctx, c_ctx, ada_w, ada_b, pre_g, post_g, w_out, a_w_in, a_lambda, a_subln_g, b_w_in, b_qk_g):
    bsz, n_lat, d = x.shape
    n_ctx = ctx.shape[1]
    depth = ada_w.shape[0]
    n_tok = n_lat + n_ctx
    assert d == D_MODEL and n_ctx == CTX_LEN and n_lat % n_ctx == 0
    n_lat_tiles = n_lat // TOKEN_TILE

    cvecs = jnp.concatenate([c, c_ctx[None, :], jnp.zeros((7 - bsz, d), jnp.float32)], axis=0)
    mods = _mods(cvecs, ada_w, ada_b).reshape(depth, 8, 1, 3 * d)

    cos_a, sin_a = _rope_tables(n_lat, n_ctx, DA_HEAD_DIM)
    cos_b, sin_b = _rope_tables(n_lat, n_ctx, GQ_HEAD_DIM)
    w_out_bf = w_out.astype(jnp.bfloat16)
    a_w_bf = a_w_in.astype(jnp.bfloat16)
    b_w_bf = b_w_in.astype(jnp.bfloat16)

    xs = jnp.concatenate([x, ctx], axis=1)
    for i in range(depth):
        last = i == depth - 1
        diff = i % 2 == 0
        j = i // 2
        lam_init = _lambda_init(i)
        if diff:
            qz, k, v, gz = _project(xs, mods[i], pre_g[i], a_w_bf[j], cos_a, sin_a, None,
                                    diff=True, n_lat_tiles=n_lat_tiles)
            lam_p, sub_g = a_lambda[j], a_subln_g[j]
        else:
            qz, k, v, gz = _project(xs, mods[i], pre_g[i], b_w_bf[j], cos_b, sin_b, b_qk_g[j],
                                    diff=False, n_lat_tiles=n_lat_tiles)
            lam_p, sub_g = None, None
        og = _attention(qz, k, v, gz, lam_p, sub_g, diff=diff, lam_init=lam_init, n_lat=n_lat,
                        ctx_queries=False, name="attn_diff" if diff else "attn_gqa")
        if last:
            return _out_project(og, w_out_bf[i], mods[i], post_g[i], xs,
                                n_tok_out=n_lat, n_lat_tiles=n_lat_tiles)
        ogc = _attention(qz, k, v, gz, lam_p, sub_g, diff=diff, lam_init=lam_init, n_lat=n_lat,
                         ctx_queries=True, name="attn_diff_ctx" if diff else "attn_gqa_ctx")
        og_all = jnp.concatenate([og, ogc], axis=1)
        xs = _out_project(og_all, w_out_bf[i], mods[i], post_g[i], xs,
                          n_tok_out=n_tok, n_lat_tiles=n_lat_tiles)
    return xs[:, :n_lat]
```

```python
import functools
import math

import jax
import jax.numpy as jnp
from jax import lax
from jax.experimental import pallas as pl
from jax.experimental.pallas import tpu as pltpu

D_MODEL = 1024
CTX_LEN = 256
GRID_W = 64
DA_HEAD_DIM = 64
DA_HEADS = 8
GQ_HEAD_DIM = 128
GQ_HEADS = 8
GQ_KV_HEADS = 2
GQ_GROUP = GQ_HEADS // GQ_KV_HEADS
ROPE_THETA = 10000.0
NORM_EPS = 1e-6

LANES = 128
TOKEN_TILE = 256
SOFTMAX_COLS = 512
Q_ROWS_DIFF = SOFTMAX_COLS // 2
Q_ROWS_GQA = SOFTMAX_COLS // GQ_GROUP
KV_CHUNKS = 6
TILES_PER_STEP_DIFF = 32
TILES_PER_STEP_GQA = 16
VMEM_LIMIT = 48 * 1024 * 1024
LOG2E = math.log2(math.e)

_NT = (((1,), (1,)), ((), ()))


def _lambda_init(layer_idx):
    return 0.8 - 0.6 * math.exp(-0.3 * layer_idx)


def _silu(x):
    return x / (1.0 + jnp.exp(-x))


def _rms(x):
    return x * lax.rsqrt(jnp.mean(x * x, axis=-1, keepdims=True) + NORM_EPS)


def _mods_kernel(cv_ref, w_ref, b_ref, o_ref):
    sc = _silu(cv_ref[...])
    o_ref[0] = jnp.dot(sc, w_ref[0], precision=lax.Precision.HIGHEST,
                       preferred_element_type=jnp.float32) + b_ref[0]


def _mods(cvecs, ada_w, ada_b):
    depth, d, d3 = ada_w.shape
    rows = cvecs.shape[0]
    tn = 1024
    return pl.pallas_call(
        _mods_kernel,
        grid=(depth, d3 // tn),
        in_specs=[
            pl.BlockSpec((rows, d), lambda i, j: (0, 0)),
            pl.BlockSpec((1, d, tn), lambda i, j: (i, 0, j)),
            pl.BlockSpec((1, 1, tn), lambda i, j: (i, 0, j)),
        ],
        out_specs=pl.BlockSpec((1, rows, tn), lambda i, j: (i, 0, j)),
        out_shape=jax.ShapeDtypeStruct((depth, rows, d3), jnp.float32),
        compiler_params=pltpu.CompilerParams(
            dimension_semantics=("arbitrary", "arbitrary"),
            vmem_limit_bytes=VMEM_LIMIT),
        name="adaln_mods",
    )(cvecs, ada_w, ada_b.reshape(depth, 1, d3))


def _rope_chunk(xc, cos, sin_signed, first_half, half):
    if 2 * half == LANES:
        partner = pltpu.roll(xc, half, 1)
    else:
        partner = jnp.where(first_half, pltpu.roll(xc, LANES - half, 1), pltpu.roll(xc, half, 1))
    return xc * cos + partner * sin_signed


def _proj_kernel(*refs, diff, q_scale):
    if diff:
        (x_ref, mod_ref, pg_ref, w_ref, cos_ref, sin_ref,
         q_ref, k_ref, v_ref, gz_ref) = refs
    else:
        (x_ref, mod_ref, pg_ref, w_ref, cos_ref, sin_ref, qkg_ref,
         q_ref, k_ref, v_ref, gz_ref) = refs
    d = D_MODEL
    tm = x_ref.shape[1]
    x = x_ref[0]
    mod = mod_ref[0]
    shift, scale = mod[:, 0:d], mod[:, d:2 * d]
    h = _rms(x) * pg_ref[...] * (1.0 + scale) + shift
    hb = h.astype(jnp.bfloat16)

    cos = cos_ref[...]
    sin = sin_ref[...]
    lane = lax.broadcasted_iota(jnp.int32, (1, LANES), 1)
    if diff:
        half = DA_HEAD_DIM // 2
        first_half = (lane % DA_HEAD_DIM) < half
        lo = lane < DA_HEAD_DIM
        nq, nk, nv = DA_HEADS, DA_HEADS, DA_HEADS
    else:
        half = GQ_HEAD_DIM // 2
        first_half = None
        nq, nk, nv = GQ_HEADS, GQ_KV_HEADS, GQ_KV_HEADS
    col = 0

    pq = jnp.dot(hb, w_ref[:, col:col + nq * LANES], preferred_element_type=jnp.float32)
    col += nq * LANES
    for c in range(nq):
        xc = pq[:, c * LANES:(c + 1) * LANES]
        if not diff:
            xc = _rms(xc) * qkg_ref[0:1, :]
        r = _rope_chunk(xc, cos, sin, first_half, half) * q_scale
        if diff:
            sl = slice(c * LANES, (c + 1) * LANES)
            q_ref[0, 0, 0:tm, sl] = jnp.where(lo, r, 0.0).astype(jnp.bfloat16)
            q_ref[0, 0, tm:2 * tm, sl] = jnp.where(lo, 0.0, r).astype(jnp.bfloat16)
        else:
            rb = r.astype(jnp.bfloat16)
            g, hh = divmod(c, GQ_GROUP)
            for ii in range(tm // Q_ROWS_GQA):
                q_ref[0, ii, g, hh * Q_ROWS_GQA:(hh + 1) * Q_ROWS_GQA, :] = (
                    rb[ii * Q_ROWS_GQA:(ii + 1) * Q_ROWS_GQA])

    pk = jnp.dot(hb, w_ref[:, col:col + nk * LANES], preferred_element_type=jnp.float32)
    col += nk * LANES
    for c in range(nk):
        xc = pk[:, c * LANES:(c + 1) * LANES]
        if not diff:
            xc = _rms(xc) * qkg_ref[1:2, :]
        r = _rope_chunk(xc, cos, sin, first_half, half)
        k_ref[0, :, c * LANES:(c + 1) * LANES] = r.astype(jnp.bfloat16)

    pv = jnp.dot(hb, w_ref[:, col:col + nv * LANES], preferred_element_type=jnp.float32)
    col += nv * LANES
    v_ref[0] = pv.astype(jnp.bfloat16)

    pz = jnp.dot(hb, w_ref[:, col:col + d], preferred_element_type=jnp.float32)
    gz_ref[0] = _silu(pz).astype(jnp.bfloat16)


def _mod_index(n_lat_tiles, n_batch):
    def index(b, t):
        return (jnp.where(t < n_lat_tiles, b, n_batch), 0, 0)
    return index


def _project(xs, mods_i, pre_g_i, w_in_bf16, cos, sin, qk_g, *, diff, n_lat_tiles):
    bsz, n_tok, d = xs.shape
    tm = TOKEN_TILE
    assert tm == Q_ROWS_DIFF and tm % Q_ROWS_GQA == 0
    width = w_in_bf16.shape[1]
    q_scale = (DA_HEAD_DIM if diff else GQ_HEAD_DIM) ** -0.5 * LOG2E
    tok = lambda w: pl.BlockSpec((1, tm, w), lambda b, t: (b, t, 0))
    in_specs = [
        tok(d),
        pl.BlockSpec((1, 1, 3 * d), _mod_index(n_lat_tiles, bsz)),
        pl.BlockSpec((1, d), lambda b, t: (0, 0)),
        pl.BlockSpec((d, width), lambda b, t: (0, 0)),
        pl.BlockSpec((tm, LANES), lambda b, t: (t, 0)),
        pl.BlockSpec((tm, LANES), lambda b, t: (t, 0)),
    ]
    args = [xs, mods_i, pre_g_i.reshape(1, d), w_in_bf16, cos, sin]
    bf = lambda w: jax.ShapeDtypeStruct((bsz, n_tok, w), jnp.bfloat16)
    kvw = (DA_HEADS if diff else GQ_KV_HEADS) * LANES
    if diff:
        q_shape = jax.ShapeDtypeStruct((bsz, n_tok // tm, SOFTMAX_COLS, d), jnp.bfloat16)
        q_spec = pl.BlockSpec((1, 1, SOFTMAX_COLS, d), lambda b, t: (b, t, 0, 0))
    else:
        per = tm // Q_ROWS_GQA
        in_specs.append(pl.BlockSpec((2, LANES), lambda b, t: (0, 0)))
        args.append(qk_g)
        q_shape = jax.ShapeDtypeStruct(
            (bsz, n_tok // Q_ROWS_GQA, GQ_KV_HEADS, SOFTMAX_COLS, LANES), jnp.bfloat16)
        q_spec = pl.BlockSpec((1, per, GQ_KV_HEADS, SOFTMAX_COLS, LANES),
                              lambda b, t: (b, t, 0, 0, 0))
    return pl.pallas_call(
        functools.partial(_proj_kernel, diff=diff, q_scale=q_scale),
        grid=(bsz, n_tok // tm),
        in_specs=in_specs,
        out_specs=[q_spec, tok(kvw), tok(kvw), tok(d)],
        out_shape=[q_shape, bf(kvw), bf(kvw), bf(d)],
        compiler_params=pltpu.CompilerParams(
            dimension_semantics=("arbitrary", "arbitrary"),
            vmem_limit_bytes=VMEM_LIMIT),
        name="proj_diff" if diff else "proj_gqa",
    )(*args)


def _attn_kernel(*refs, diff, n_tiles, n_chunks, lam_init):
    if diff:
        (q_ref, k_ref, v_ref, gz_ref, lam_ref, g_ref,
         o_ref, vt_ref, s_ref, m_ref, l_ref, acc_ref) = refs
        rows = Q_ROWS_DIFF
        q_tile = lambda i: q_ref[0, i]
    else:
        (q_ref, k_ref, v_ref, gz_ref,
         o_ref, vt_ref, s_ref, m_ref, l_ref, acc_ref) = refs
        rows = Q_ROWS_GQA
        q_tile = lambda i: q_ref[0, i, 0]
    kv_chunk = vt_ref.shape[2]

    @pl.when(pl.program_id(2) == 0)
    def _transpose_values():
        for c in range(n_chunks):
            vc = v_ref[0, c * kv_chunk:(c + 1) * kv_chunk, :].astype(jnp.float32)
            vt_ref[c] = vc.T.astype(jnp.bfloat16)

    def reset():
        m_ref[...] = jnp.full(m_ref.shape, -jnp.inf, jnp.float32)
        l_ref[...] = jnp.zeros(l_ref.shape, jnp.float32)
        acc_ref[...] = jnp.zeros(acc_ref.shape, jnp.float32)

    def scores(i, c, slot):
        off = c * kv_chunk
        if not isinstance(off, int):
            off = pl.multiple_of(off, 16)
        s_ref[slot] = lax.dot_general(k_ref[0, pl.ds(off, kv_chunk), :], q_tile(i), _NT,
                                      preferred_element_type=jnp.float32)

    def softmax_pv(c, slot):
        s = s_ref[slot]
        m_old = m_ref[...]
        m_new = jnp.maximum(m_old, jnp.max(s, axis=0, keepdims=True))
        alpha = jnp.exp2(m_old - m_new)
        p = jnp.exp2(s - m_new)
        l_ref[...] = alpha * l_ref[...] + jnp.sum(p, axis=0, keepdims=True)
        acc_ref[...] = alpha * acc_ref[...] + jnp.dot(
            vt_ref[c], p.astype(jnp.bfloat16), preferred_element_type=jnp.float32)
        m_ref[...] = m_new

    def finalize(i):
        r0 = i * rows
        if not isinstance(r0, int):
            r0 = pl.multiple_of(r0, rows)
        o_t = acc_ref[...] / l_ref[...]
        if diff:
            lp = lam_ref[...]
            lam = (jnp.exp(jnp.sum(lp[0:1] * lp[1:2], axis=-1, keepdims=True))
                   - jnp.exp(jnp.sum(lp[2:3] * lp[3:4], axis=-1, keepdims=True)) + lam_init)
            o_t = o_t[:, 0:rows] - lam * o_t[:, rows:2 * rows]
            o_t = o_t * lax.rsqrt(jnp.mean(o_t * o_t, axis=0, keepdims=True) + NORM_EPS)
            o_t = o_t * g_ref[...] * (1.0 - lam_init)
            gate = gz_ref[0, pl.ds(r0, rows), :].astype(jnp.float32)
            o_ref[0, pl.ds(r0, rows), :] = (o_t.T * gate).astype(jnp.bfloat16)
        else:
            o = o_t.T
            for h in range(GQ_GROUP):
                sl = slice(h * LANES, (h + 1) * LANES)
                gate = gz_ref[0, pl.ds(r0, rows), sl].astype(jnp.float32)
                o_ref[0, pl.ds(r0, rows), sl] = (o[h * rows:(h + 1) * rows] * gate).astype(jnp.bfloat16)

    reset()
    scores(0, 0, 0)
    if n_chunks == 1:
        assert n_tiles == 1
        softmax_pv(0, 0)
        finalize(0)
        return

    assert n_chunks % 2 == 0

    def tile(i, carry):
        for c in range(n_chunks):
            if c + 1 < n_chunks:
                scores(i, c + 1, (c + 1) % 2)
            else:
                scores(jnp.minimum(i + 1, n_tiles - 1), 0, 0)
            softmax_pv(c, c % 2)
        finalize(i)
        reset()
        return carry

    lax.fori_loop(0, n_tiles, tile, 0)


def _attention(qz, k, v, gz, lam_p, subln_g, *, diff, lam_init, n_lat, ctx_queries, name):
    bsz, n_tok, d = gz.shape
    n_ctx = n_tok - n_lat
    if diff:
        q_rows, o_cols, n_kv, tiles = Q_ROWS_DIFF, LANES, DA_HEADS, TILES_PER_STEP_DIFF
    else:
        q_rows, o_cols, n_kv, tiles = Q_ROWS_GQA, SOFTMAX_COLS, GQ_KV_HEADS, TILES_PER_STEP_GQA
    if ctx_queries:
        tiles, n_chunks, kv_len = 1, 1, n_ctx
        tile_off, n_steps = n_lat // q_rows, n_ctx // q_rows
        kv_block = n_lat // n_ctx
    else:
        n_chunks, kv_len = KV_CHUNKS, n_tok
        tiles = min(tiles, n_lat // q_rows)
        assert n_lat % (tiles * q_rows) == 0 and n_tok % (KV_CHUNKS * LANES) == 0
        tile_off, n_steps = 0, n_lat // (tiles * q_rows)
        kv_block = 0
    kv_chunk = kv_len // n_chunks
    if diff:
        qspec = pl.BlockSpec((1, tiles, SOFTMAX_COLS, LANES),
                             lambda b, h, t: (b, tile_off + t, 0, h))
    else:
        qspec = pl.BlockSpec((1, tiles, 1, SOFTMAX_COLS, LANES),
                             lambda b, h, t: (b, tile_off + t, h, 0, 0))
    kvspec = pl.BlockSpec((1, kv_len, LANES), lambda b, h, t: (b, kv_block, h))
    gzspec = pl.BlockSpec((1, tiles * q_rows, o_cols), lambda b, h, t: (b, tile_off + t, h))
    in_specs = [qspec, kvspec, kvspec, gzspec]
    args = [qz, k, v, gz]
    if diff:
        in_specs += [pl.BlockSpec((4, DA_HEAD_DIM), lambda b, h, t: (0, 0)),
                     pl.BlockSpec((LANES, 1), lambda b, h, t: (0, 0))]
        args += [lam_p, subln_g.reshape(LANES, 1)]
    return pl.pallas_call(
        functools.partial(_attn_kernel, diff=diff, n_tiles=tiles, n_chunks=n_chunks,
                          lam_init=lam_init),
        grid=(bsz, n_kv, n_steps),
        in_specs=in_specs,
        out_specs=pl.BlockSpec((1, tiles * q_rows, o_cols), lambda b, h, t: (b, t, h)),
        out_shape=jax.ShapeDtypeStruct((bsz, n_steps * tiles * q_rows, d), jnp.bfloat16),
        scratch_shapes=[
            pltpu.VMEM((n_chunks, LANES, kv_chunk), jnp.bfloat16),
            pltpu.VMEM((2, kv_chunk, SOFTMAX_COLS), jnp.float32),
            pltpu.VMEM((1, SOFTMAX_COLS), jnp.float32),
            pltpu.VMEM((1, SOFTMAX_COLS), jnp.float32),
            pltpu.VMEM((LANES, SOFTMAX_COLS), jnp.float32),
        ],
        compiler_params=pltpu.CompilerParams(
            dimension_semantics=("arbitrary", "arbitrary", "arbitrary"),
            vmem_limit_bytes=VMEM_LIMIT),
        name=name,
    )(*args)


def _out_kernel(og_ref, w_ref, mod_ref, pg_ref, x_ref, o_ref):
    d = D_MODEL
    y = jnp.dot(og_ref[0], w_ref[...], preferred_element_type=jnp.float32)
    gate = mod_ref[0][:, 2 * d:3 * d]
    o_ref[0] = x_ref[0] + gate * (_rms(y) * pg_ref[...])


def _out_project(og, w_out_bf16, mods_i, post_g_i, xs, *, n_tok_out, n_lat_tiles):
    bsz, _, d = xs.shape
    tm = TOKEN_TILE
    tok = pl.BlockSpec((1, tm, d), lambda b, t: (b, t, 0))
    return pl.pallas_call(
        _out_kernel,
        grid=(bsz, n_tok_out // tm),
        in_specs=[
            tok,
            pl.BlockSpec((d, d), lambda b, t: (0, 0)),
            pl.BlockSpec((1, 1, 3 * d), _mod_index(n_lat_tiles, bsz)),
            pl.BlockSpec((1, d), lambda b, t: (0, 0)),
            tok,
        ],
        out_specs=tok,
        out_shape=jax.ShapeDtypeStruct((bsz, n_tok_out, d), jnp.float32),
        compiler_params=pltpu.CompilerParams(
            dimension_semantics=("arbitrary", "arbitrary"),
            vmem_limit_bytes=VMEM_LIMIT),
        name="out_proj",
    )(og, w_out_bf16, mods_i, post_g_i.reshape(1, d), xs)


def _rope_tables(n_lat, n_ctx, head_dim):
    rows = jnp.repeat(jnp.arange(n_lat // GRID_W, dtype=jnp.int32), GRID_W)
    cols = jnp.tile(jnp.arange(GRID_W, dtype=jnp.int32), n_lat // GRID_W)
    axis_dim = head_dim // 2
    freqs = ROPE_THETA ** (-jnp.arange(0, axis_dim, 2, dtype=jnp.float32) / axis_dim)
    ang = jnp.concatenate([rows.astype(jnp.float32)[:, None] * freqs,
                           cols.astype(jnp.float32)[:, None] * freqs], axis=-1)
    cos, sin = jnp.cos(ang), jnp.sin(ang)
    reps = LANES // head_dim
    cos_w = jnp.tile(jnp.concatenate([cos, cos], axis=-1), (1, reps))
    sin_w = jnp.tile(jnp.concatenate([-sin, sin], axis=-1), (1, reps))
    cos_w = jnp.concatenate([cos_w, jnp.ones((n_ctx, LANES), jnp.float32)], axis=0)
    sin_w = jnp.concatenate([sin_w, jnp.zeros((n_ctx, LANES), jnp.float32)], axis=0)
    return cos_w, sin_w


def kernel(x, c, ctx, c_ctx, ada_w, ada_b, pre_g, post_g, w_out, a_w_in, a_lambda, a_subln_g, b_w_in, b_qk_g):
    bsz, n_lat, d = x.shape
    n_ctx = ctx.shape[1]
    depth = ada_w.shape[0]
    n_tok = n_lat + n_ctx
    assert d == D_MODEL and n_ctx == CTX_LEN and n_lat % n_ctx == 0
    n_lat_tiles = n_lat // TOKEN_TILE

    cvecs = jnp.concatenate([c, c_ctx[None, :], jnp.zeros((7 - bsz, d), jnp.float32)], axis=0)
    mods = _mods(cvecs, ada_w, ada_b).reshape(depth, 8, 1, 3 * d)

    cos_a, sin_a = _rope_tables(n_lat, n_ctx, DA_HEAD_DIM)
    cos_b, sin_b = _rope_tables(n_lat, n_ctx, GQ_HEAD_DIM)
    w_out_bf = w_out.astype(jnp.bfloat16)
    a_w_bf = a_w_in.astype(jnp.bfloat16)
    b_w_bf = b_w_in.astype(jnp.bfloat16)

    xs = jnp.concatenate([x, ctx], axis=1)
    for i in range(depth):
        last = i == depth - 1
        diff = i % 2 == 0
        j = i // 2
        lam_init = _lambda_init(i)
        if diff:
            qz, k, v, gz = _project(xs, mods[i], pre_g[i], a_w_bf[j], cos_a, sin_a, None,
                                    diff=True, n_lat_tiles=n_lat_tiles)
            lam_p, sub_g = a_lambda[j], a_subln_g[j]
        else:
            qz, k, v, gz = _project(xs, mods[i], pre_g[i], b_w_bf[j], cos_b, sin_b, b_qk_g[j],
                                    diff=False, n_lat_tiles=n_lat_tiles)
            lam_p, sub_g = None, None
        og = _attention(qz, k, v, gz, lam_p, sub_g, diff=diff, lam_init=lam_init, n_lat=n_lat,
                        ctx_queries=False, name="attn_diff" if diff else "attn_gqa")
        if last:
            return _out_project(og, w_out_bf[i], mods[i], post_g[i], xs,
                                n_tok_out=n_lat, n_lat_tiles=n_lat_tiles)
        ogc = _attention(qz, k, v, gz, lam_p, sub_g, diff=diff, lam_init=lam_init, n_lat=n_lat,
                         ctx_queries=True, name="attn_diff_ctx" if diff else "attn_gqa_ctx")
        og_all = jnp.concatenate([og, ogc], axis=1)
        xs = _out_project(og_all, w_out_bf[i], mods[i], post_g[i], xs,
                          n_tok_out=n_tok, n_lat_tiles=n_lat_tiles)
    return xs[:, :n_lat]
```
